```python
import math
import jax
import jax.numpy as jnp
from jax import lax
import numpy as np


D_MODEL = 2048
BATCH = 4
SEQ = 8192
DEPTH = 4

N_A_LAYERS = DEPTH // 2
N_B_LAYERS = DEPTH - N_A_LAYERS
N_DENSE = (DEPTH + 1) // 2
N_MOE = DEPTH // 2

RET_HEADS = D_MODEL // 256
RET_DK = D_MODEL // RET_HEADS
RET_DV = 2 * RET_DK
RET_QK_WIDTH = RET_HEADS * RET_DK
RET_V_WIDTH = RET_HEADS * RET_DV
RET_CHUNK = 128
ROPE_BASE = 10000.0

FOX_HEADS = 16
FOX_DH = D_MODEL // FOX_HEADS
FOX_WIDTH = FOX_HEADS * FOX_DH
FOX_BLOCK = 128

FFN_DIM = 256 * ((8 * D_MODEL // 3 + 255) // 256)
N_EXPERTS = 8
TOP_K = 2

EPS = 1e-6

kernel_name = "retention_fox_yoco_moe_adaln_trunk"


def rms_norm(x, gain=None):
    xf = x.astype(jnp.float32)
    y = xf * lax.rsqrt(jnp.mean(xf * xf, axis=-1, keepdims=True) + EPS)
    if gain is not None:
        y = y * gain.astype(jnp.float32)
    return y.astype(x.dtype)


def modulate(x, gain, shift, scale):
    return rms_norm(x, gain) * (1.0 + scale[:, None, :]) + shift[:, None, :]


def rotary(x):
    s, d = x.shape[1], x.shape[-1]
    half = d // 2
    inv = 1.0 / (ROPE_BASE ** (jnp.arange(half, dtype=jnp.float32) / half))
    ang = jnp.arange(s, dtype=jnp.float32)[:, None] * inv[None, :]
    cos = jnp.cos(ang)[None, :, None, :]
    sin = jnp.sin(ang)[None, :, None, :]
    x1, x2 = x[..., :half], x[..., half:]
    return jnp.concatenate([x1 * cos - x2 * sin, x1 * sin + x2 * cos], axis=-1).astype(x.dtype)


def retention_mixer(u, w_in, w_out):
    b, s, _ = u.shape
    proj = u @ w_in
    q, k, v, g = jnp.split(proj, [RET_QK_WIDTH, 2 * RET_QK_WIDTH, 2 * RET_QK_WIDTH + RET_V_WIDTH], axis=-1)
    q = rotary(q.reshape(b, s, RET_HEADS, RET_DK))
    k = rotary(k.reshape(b, s, RET_HEADS, RET_DK)) * (RET_DK ** -0.5)
    v = v.reshape(b, s, RET_HEADS, RET_DV)
    nc = s // RET_CHUNK

    def to_chunks(t):
        return t.reshape(b, nc, RET_CHUNK, RET_HEADS, t.shape[-1]).transpose(1, 0, 3, 2, 4)

    log_gamma = jnp.log1p(-jnp.exp2(-5.0 - jnp.arange(RET_HEADS, dtype=jnp.float32)))
    idx = jnp.arange(RET_CHUNK, dtype=jnp.float32)
    dist = idx[:, None] - idx[None, :]
    d_local = jnp.where(dist >= 0, jnp.exp(log_gamma[:, None, None] * jnp.maximum(dist, 0.0)), 0.0)
    xi = jnp.exp(log_gamma[:, None] * (idx + 1.0))[..., None]
    zeta = jnp.exp(log_gamma[:, None] * (RET_CHUNK - 1.0 - idx))[..., None]
    gamma_c = jnp.exp(log_gamma * RET_CHUNK)[:, None, None]

    def step(state, inp):
        qc, kc, vc = inp
        scores = jnp.einsum('bhtd,bhsd->bhts', qc, kc) * d_local
        inner = jnp.einsum('bhts,bhsv->bhtv', scores, vc)
        cross = jnp.einsum('bhtd,bhdv->bhtv', qc, state) * xi
        new_state = gamma_c * state + jnp.einsum('bhsd,bhsv->bhdv', kc, vc * zeta)
        return new_state.astype(jnp.float32), (inner + cross).astype(jnp.float32)

    state0 = jnp.zeros((b, RET_HEADS, RET_DK, RET_DV), jnp.float32)
    _, out = lax.scan(step, state0, (to_chunks(q), to_chunks(k), to_chunks(v)))
    out = out.transpose(1, 0, 3, 2, 4).reshape(b, s, RET_HEADS, RET_DV)
    out = rms_norm(out).reshape(b, s, RET_V_WIDTH)
    y = jax.nn.swish(g) * out.astype(g.dtype)
    return y @ w_out


def shared_kv(h, cond, norm_gain, mod_w, mod_b, w_kvf, forget_bias, k_gain):
    b, s, _ = h.shape
    shift, scale = jnp.split(cond @ mod_w + mod_b, 2, axis=-1)
    u = modulate(h, norm_gain, shift, scale)
    proj = u @ w_kvf
    k, v, f = jnp.split(proj, [FOX_WIDTH, 2 * FOX_WIDTH], axis=-1)
    k = rms_norm(k.reshape(b, s, FOX_HEADS, FOX_DH), k_gain).transpose(0, 2, 1, 3)
    v = v.reshape(b, s, FOX_HEADS, FOX_DH).transpose(0, 2, 1, 3)
    log_f = jax.nn.log_sigmoid((f + forget_bias).astype(jnp.float32))
    cum = lax.cumsum(log_f, axis=1).transpose(0, 2, 1)
    return k, v, cum


def fox_mixer(u, k, v, cum, w_q, q_gain, w_out):
    b, s, _ = u.shape
    q = rms_norm((u @ w_q).reshape(b, s, FOX_HEADS, FOX_DH), q_gain) * (FOX_DH ** -0.5)
    nb = s // FOX_BLOCK
    q_blocks = q.reshape(b, nb, FOX_BLOCK, FOX_HEADS, FOX_DH).transpose(1, 0, 3, 2, 4)
    cum_blocks = cum.reshape(b, FOX_HEADS, nb, FOX_BLOCK).transpose(2, 0, 1, 3)
    starts = jnp.arange(nb, dtype=jnp.int32) * FOX_BLOCK
    key_pos = jnp.arange(s, dtype=jnp.int32)

    def block(inp):
        qb, cb, start = inp
        logits = jnp.einsum('bhtd,bhsd->bhts', qb, k).astype(jnp.float32)
        logits = logits + (cb[..., None] - cum[:, :, None, :])
        q_pos = start + jnp.arange(FOX_BLOCK, dtype=jnp.int32)
        mask = key_pos[None, :] <= q_pos[:, None]
        p = jax.nn.softmax(jnp.where(mask, logits, -jnp.inf), axis=-1)
        return jnp.einsum('bhts,bhsd->bhtd', p.astype(v.dtype), v)

    out = lax.map(block, (q_blocks, cum_blocks, starts))
    out = out.transpose(1, 0, 3, 2, 4).reshape(b, s, FOX_WIDTH)
    return out @ w_out


def swiglu(u, w13, w2):
    a, g = jnp.split(u @ w13, 2, axis=-1)
    return (jax.nn.silu(a) * g) @ w2


def moe_swiglu(u, router, w13, w2):
    logits = (u @ router).astype(jnp.float32)
    top_val, top_idx = lax.top_k(logits, TOP_K)
    top_w = jax.nn.softmax(top_val, axis=-1)
    combine = jnp.sum(jax.nn.one_hot(top_idx, N_EXPERTS, dtype=jnp.float32) * top_w[..., None], axis=-2)
    combine = combine.astype(u.dtype)
    out = jnp.zeros_like(u)
    for e in range(N_EXPERTS):
        out = out + combine[..., e:e + 1] * swiglu(u, w13[e], w2[e])
    return out


def setup_inputs(seed: int = 0) -> dict:
    key = jax.random.key(seed)
    ks = jax.random.split(key, 24)
    f32 = jnp.float32
    D = D_MODEL

    def nrm(k, shape, scale):
        return jax.random.normal(k, shape, f32) * scale

    ret_in_width = 2 * RET_QK_WIDTH + 2 * RET_V_WIDTH
    return {
        'x': nrm(ks[0], (BATCH, SEQ, D), 1.0),
        'c': nrm(ks[1], (BATCH, D), 1.0),
        'norm_gain': 1.0 + nrm(ks[2], (DEPTH, 2, D), 0.02),
        'mod_w': nrm(ks[3], (DEPTH, D, 6 * D), 0.5 * D ** -0.5),
        'mod_b': nrm(ks[4], (DEPTH, 6 * D), 0.02),
        'ret_w_in': nrm(ks[5], (N_A_LAYERS, D, ret_in_width), D ** -0.5),
        'ret_w_out': nrm(ks[6], (N_A_LAYERS, RET_V_WIDTH, D), RET_V_WIDTH ** -0.5),
        'fox_w_q': nrm(ks[7], (N_B_LAYERS, D, FOX_WIDTH), D ** -0.5),
        'fox_q_gain': 1.0 + nrm(ks[8], (N_B_LAYERS, FOX_DH), 0.02),
        'fox_w_out': nrm(ks[9], (N_B_LAYERS, FOX_WIDTH, D), FOX_WIDTH ** -0.5),
        'kv_norm_gain': 1.0 + nrm(ks[10], (D,), 0.02),
        'kv_mod_w': nrm(ks[11], (D, 2 * D), 0.5 * D ** -0.5),
        'kv_mod_b': nrm(ks[12], (2 * D,), 0.02),
        'kv_w': nrm(ks[13], (D, 2 * FOX_WIDTH + FOX_HEADS), D ** -0.5),
        'kv_forget_bias': jnp.linspace(1.0, 5.0, FOX_HEADS, dtype=f32) + nrm(ks[14], (FOX_HEADS,), 0.01),
        'kv_k_gain': 1.0 + nrm(ks[15], (FOX_DH,), 0.02),
        'ffn_w13': nrm(ks[16], (N_DENSE, D, 2 * FFN_DIM), D ** -0.5),
        'ffn_w2': nrm(ks[17], (N_DENSE, FFN_DIM, D), FFN_DIM ** -0.5),
        'moe_router': nrm(ks[18], (N_MOE, D, N_EXPERTS), D ** -0.5),
        'moe_w13': nrm(ks[19], (N_MOE, N_EXPERTS, D, 2 * FFN_DIM), D ** -0.5),
        'moe_w2': nrm(ks[20], (N_MOE, N_EXPERTS, FFN_DIM, D), FFN_DIM ** -0.5),
    }


def reference(x, c, norm_gain, mod_w, mod_b, ret_w_in, ret_w_out, fox_w_q, fox_q_gain, fox_w_out,
              kv_norm_gain, kv_mod_w, kv_mod_b, kv_w, kv_forget_bias, kv_k_gain,
              ffn_w13, ffn_w2, moe_router, moe_w13, moe_w2):
    cond = jax.nn.silu(c)
    h = x
    k_sh = v_sh = cum_sh = None
    for i in range(DEPTH):
        mod = cond @ mod_w[i] + mod_b[i]
        sh1, sc1, g1, sh2, sc2, g2 = jnp.split(mod, 6, axis=-1)
        u = modulate(h, norm_gain[i, 0], sh1, sc1)
        if i < N_A_LAYERS:
            y = retention_mixer(u, ret_w_in[i], ret_w_out[i])
        else:
            j = i - N_A_LAYERS
            y = fox_mixer(u, k_sh, v_sh, cum_sh, fox_w_q[j], fox_q_gain[j], fox_w_out[j])
        h = h + g1[:, None, :] * y
        u = modulate(h, norm_gain[i, 1], sh2, sc2)
        if i % 2 == 0:
            y = swiglu(u, ffn_w13[i // 2], ffn_w2[i // 2])
        else:
            y = moe_swiglu(u, moe_router[i // 2], moe_w13[i // 2], moe_w2[i // 2])
        h = h + g2[:, None, :] * y
        if i == N_A_LAYERS - 1:
            k_sh, v_sh, cum_sh = shared_kv(h, cond, kv_norm_gain, kv_mod_w, kv_mod_b, kv_w,
                                           kv_forget_bias, kv_k_gain)
    return h
```

```python
import functools
import math

import jax
import jax.numpy as jnp
from jax import lax
from jax.experimental import pallas as pl
from jax.experimental.pallas import tpu as pltpu

F32 = jnp.float32
BF16 = jnp.bfloat16
HIGHEST = lax.Precision.HIGHEST

EPS = 1e-6
ROPE_BASE = 10000.0
RET_DK = 256
RET_DV = 2 * RET_DK
FOX_DH = 128
TOP_K = 2
LANES = 128
SUBLANES_BF16 = 16
VMEM_LIMIT_CAP = 56 * 1024 * 1024


def _params(semantics, vmem_bytes):
    limit = int(min(max(2 * vmem_bytes, 16 * 1024 * 1024), VMEM_LIMIT_CAP))
    return pltpu.CompilerParams(dimension_semantics=semantics, vmem_limit_bytes=limit)


def _tile(n, pref):
    t = min(n, pref)
    assert n % t == 0, (n, pref)
    return t


def _norm_mod(h, gain, scale, shift):
    ms = jnp.mean(h * h, axis=-1, keepdims=True)
    return h * lax.rsqrt(ms + EPS) * gain * (1.0 + scale) + shift


def _mod_kernel(c_ref, w_ref, b_ref, o_ref):
    c = c_ref[...]
    cond = c * jax.nn.sigmoid(c)
    o_ref[...] = jnp.dot(cond, w_ref[...], precision=HIGHEST, preferred_element_type=F32) + b_ref[...]


def _mod_call(c8, w, b):
    L, D, N = w.shape
    tn = _tile(N, 1024)
    return pl.pallas_call(
        _mod_kernel,
        grid=(L, N // tn),
        in_specs=[
            pl.BlockSpec((8, D), lambda l, j: (0, 0)),
            pl.BlockSpec((None, D, tn), lambda l, j: (l, 0, j)),
            pl.BlockSpec((None, 1, tn), lambda l, j: (l, 0, j)),
        ],
        out_specs=pl.BlockSpec((None, 8, tn), lambda l, j: (l, 0, j)),
        out_shape=jax.ShapeDtypeStruct((L, 8, N), F32),
        compiler_params=_params(("arbitrary", "arbitrary"), 2 * D * tn * 4),
        name="adaln_mod",
    )(c8, w, b.reshape(L, 1, N))


def _nm_matmul_kernel(h_ref, gain_ref, scale_ref, shift_ref, w_ref, hg_ref, o_ref, u_ref,
                      *, n_norm_tiles, head_scale):
    j = pl.program_id(1)

    @pl.when(j == 0)
    def _():
        u_ref[...] = _norm_mod(h_ref[...], gain_ref[...], scale_ref[...], shift_ref[...]).astype(BF16)

    acc = jnp.dot(u_ref[...], w_ref[...], preferred_element_type=F32)

    if n_norm_tiles == 0:
        o_ref[...] = acc.astype(o_ref.dtype)
    else:
        @pl.when(j < n_norm_tiles)
        def _():
            hg = hg_ref[...] * head_scale
            for c in range(acc.shape[1] // FOX_DH):
                blk = acc[:, c * FOX_DH:(c + 1) * FOX_DH]
                ms = jnp.mean(blk * blk, axis=-1, keepdims=True)
                o_ref[:, c * FOX_DH:(c + 1) * FOX_DH] = (blk * lax.rsqrt(ms + EPS) * hg).astype(o_ref.dtype)

        @pl.when(j >= n_norm_tiles)
        def _():
            o_ref[...] = acc.astype(o_ref.dtype)


def _nm_matmul(h, gain, scale, shift, w, seq, *, head_gain=None, norm_cols=0, head_scale=1.0,
               tm=1024, tn=1024):
    T, D = h.shape
    N = w.shape[1]
    tm, tn = _tile(seq, tm), _tile(N, tn)
    assert norm_cols % tn == 0
    if head_gain is None:
        head_gain = jnp.ones((FOX_DH,), F32)
    rows_per_b = seq // tm
    vmem = 2 * tm * D * 4 + 2 * D * tn * 2 + 2 * tm * tn * 2 + tm * D * 2
    kern = functools.partial(_nm_matmul_kernel, n_norm_tiles=norm_cols // tn, head_scale=head_scale)
    return pl.pallas_call(
        kern,
        grid=(T // tm, N // tn),
        in_specs=[
            pl.BlockSpec((tm, D), lambda i, j: (i, 0)),
            pl.BlockSpec((1, D), lambda i, j: (0, 0)),
            pl.BlockSpec((None, 1, D), lambda i, j: (i // rows_per_b, 0, 0)),
            pl.BlockSpec((None, 1, D), lambda i, j: (i // rows_per_b, 0, 0)),
            pl.BlockSpec((D, tn), lambda i, j: (0, j)),
            pl.BlockSpec((1, FOX_DH), lambda i, j: (0, 0)),
        ],
        out_specs=pl.BlockSpec((tm, tn), lambda i, j: (i, j)),
        out_shape=jax.ShapeDtypeStruct((T, N), BF16),
        scratch_shapes=[pltpu.VMEM((tm, D), BF16)],
        compiler_params=_params(("arbitrary", "arbitrary"), vmem),
        name="norm_mod_proj",
    )(h, gain.reshape(1, D), scale, shift, w, head_gain.reshape(1, FOX_DH))


def _proj_residual_kernel(y_ref, w_ref, h_ref, gate_ref, o_ref):
    acc = jnp.dot(y_ref[...], w_ref[...], preferred_element_type=F32)
    o_ref[...] = h_ref[...] + gate_ref[...] * acc


def _proj_residual(y, w, h, gate, seq, *, tm=1024, tn=512):
    T, K = y.shape
    D = w.shape[1]
    tm, tn = _tile(seq, tm), _tile(D, tn)
    rows_per_b = seq // tm
    vmem = 2 * tm * K * 2 + 2 * K * tn * 2 + 4 * tm * tn * 4
    return pl.pallas_call(
        _proj_residual_kernel,
        grid=(T // tm, D // tn),
        in_specs=[
            pl.BlockSpec((tm, K), lambda i, j: (i, 0)),
            pl.BlockSpec((K, tn), lambda i, j: (0, j)),
            pl.BlockSpec((tm, tn), lambda i, j: (i, j)),
            pl.BlockSpec((None, 1, tn), lambda i, j: (i // rows_per_b, 0, j)),
        ],
        out_specs=pl.BlockSpec((tm, tn), lambda i, j: (i, j)),
        out_shape=jax.ShapeDtypeStruct((T, D), F32),
        compiler_params=_params(("arbitrary", "arbitrary"), vmem),
        name="proj_residual",
    )(y, w, h, gate)


def _rotary(x, cos, sin):
    half = x.shape[-1] // 2
    x1, x2 = x[:, :half], x[:, half:]
    return jnp.concatenate([x1 * cos - x2 * sin, x1 * sin + x2 * cos], axis=-1)


def _retention_kernel(lg_ref, q_ref, k_ref, v_ref, g_ref, cos_ref, sin_ref, o_ref, state_ref,
                      *, chunk, n_chunks):
    @pl.when(pl.program_id(2) == 0)
    def _():
        state_ref[...] = jnp.zeros_like(state_ref)

    lg = lg_ref[pl.program_id(1)]
    row = lax.broadcasted_iota(jnp.int32, (chunk, chunk), 0)
    col = lax.broadcasted_iota(jnp.int32, (chunk, chunk), 1)
    dist = (row - col).astype(F32)
    d_local = jnp.where(dist >= 0, jnp.exp(lg * jnp.maximum(dist, 0.0)), 0.0)
    t = lax.broadcasted_iota(jnp.int32, (chunk, 1), 0).astype(F32)
    xi = jnp.exp(lg * (t + 1.0))
    zeta = jnp.exp(lg * (chunk - 1.0 - t))
    gamma_c = jnp.exp(jnp.full((1, 1), lg * chunk, F32))

    def body(ci, carry):
        r0 = pl.multiple_of(ci * chunk, chunk)
        cos = cos_ref[pl.ds(r0, chunk), :]
        sin = sin_ref[pl.ds(r0, chunk), :]
        q = _rotary(q_ref[pl.ds(r0, chunk), :].astype(F32), cos, sin).astype(BF16)
        k = (_rotary(k_ref[pl.ds(r0, chunk), :].astype(F32), cos, sin) * (RET_DK ** -0.5)).astype(BF16)
        v = v_ref[pl.ds(r0, chunk), :]
        scores = lax.dot_general(q, k, (((1,), (1,)), ((), ())), preferred_element_type=F32) * d_local
        inner = jnp.dot(scores.astype(BF16), v, preferred_element_type=F32)
        state = state_ref[...]
        cross = jnp.dot(q, state.astype(BF16), preferred_element_type=F32) * xi
        vz = (v.astype(F32) * zeta).astype(BF16)
        state_ref[...] = gamma_c * state + lax.dot_general(
            k, vz, (((0,), (0,)), ((), ())), preferred_element_type=F32)
        o = inner + cross
        ms = jnp.mean(o * o, axis=-1, keepdims=True)
        g = g_ref[pl.ds(r0, chunk), :].astype(F32)
        o_ref[pl.ds(r0, chunk), :] = (g * jax.nn.sigmoid(g) * (o * lax.rsqrt(ms + EPS))).astype(o_ref.dtype)
        return carry

    lax.fori_loop(0, n_chunks, body, 0)


def _retention(proj, batch, seq, heads, *, chunk=256, rows=1024):
    T = proj.shape[0]
    chunk = _tile(seq, chunk)
    rows = _tile(seq, rows)
    nblk = seq // rows
    half = RET_DK // 2
    inv = 1.0 / (ROPE_BASE ** (jnp.arange(half, dtype=F32) / half))
    ang = jnp.arange(seq, dtype=F32)[:, None] * inv[None, :]
    cos, sin = jnp.cos(ang), jnp.sin(ang)
    log_gamma = jnp.log1p(-jnp.exp2(-5.0 - jnp.arange(heads, dtype=F32)))
    kern = functools.partial(_retention_kernel, chunk=chunk, n_chunks=rows // chunk)
    qk_blocks = heads
    v_off = 2 * heads * RET_DK // RET_DV
    return pl.pallas_call(
        kern,
        grid=(batch, heads, nblk),
        in_specs=[
            pl.BlockSpec(memory_space=pltpu.SMEM),
            pl.BlockSpec((rows, RET_DK), lambda b, h, c: (b * nblk + c, h)),
            pl.BlockSpec((rows, RET_DK), lambda b, h, c: (b * nblk + c, qk_blocks + h)),
            pl.BlockSpec((rows, RET_DV), lambda b, h, c: (b * nblk + c, v_off + h)),
            pl.BlockSpec((rows, RET_DV), lambda b, h, c: (b * nblk + c, v_off + heads + h)),
            pl.BlockSpec((rows, half), lambda b, h, c: (c, 0)),
            pl.BlockSpec((rows, half), lambda b, h, c: (c, 0)),
        ],
        out_specs=pl.BlockSpec((rows, RET_DV), lambda b, h, c: (b * nblk + c, h)),
        out_shape=jax.ShapeDtypeStruct((T, heads * RET_DV), BF16),
        scratch_shapes=[pltpu.VMEM((RET_DK, RET_DV), F32)],
        compiler_params=_params(("arbitrary", "arbitrary", "arbitrary"), 16 * 1024 * 1024),
        name="retention",
    )(log_gamma, proj, proj, proj, proj, cos, sin)


def _small_proj(h_ref, gain_ref, scale_ref, shift_ref, w_ref):
    u = _norm_mod(h_ref[...], gain_ref[...], scale_ref[...], shift_ref[...])
    return u, jnp.dot(u, w_ref[...], precision=HIGHEST, preferred_element_type=F32)


def _forget_kernel(h_ref, gain_ref, scale_ref, shift_ref, w_ref, bias_ref, o_ref, carry_ref,
                   *, tiles_per_seq):
    @pl.when(pl.program_id(0) % tiles_per_seq == 0)
    def _():
        carry_ref[...] = jnp.zeros_like(carry_ref)

    _, f = _small_proj(h_ref, gain_ref, scale_ref, shift_ref, w_ref)
    log_f = jax.nn.log_sigmoid(f + bias_ref[...])
    tm = log_f.shape[0]
    row = lax.broadcasted_iota(jnp.int32, (tm, tm), 0)
    col = lax.broadcasted_iota(jnp.int32, (tm, tm), 1)
    tri = jnp.where(col <= row, 1.0, 0.0).astype(F32)
    cum = jnp.dot(tri, log_f, precision=HIGHEST, preferred_element_type=F32) + carry_ref[...]
    o_ref[...] = cum
    carry_ref[...] = cum[tm - 1:tm, :]


def _forget_cumsum(h, gain, scale, shift, w_f, bias, seq, *, tm=512):
    T, D = h.shape
    nh = w_f.shape[1]
    tm = _tile(seq, tm)
    rows_per_b = seq // tm
    w_pad = jnp.zeros((D, LANES), F32).at[:, :nh].set(w_f)
    b_pad = jnp.zeros((1, LANES), F32).at[0, :nh].set(bias)
    kern = functools.partial(_forget_kernel, tiles_per_seq=rows_per_b)
    return pl.pallas_call(
        kern,
        grid=(T // tm,),
        in_specs=[
            pl.BlockSpec((tm, D), lambda i: (i, 0)),
            pl.BlockSpec((1, D), lambda i: (0, 0)),
            pl.BlockSpec((None, 1, D), lambda i: (i // rows_per_b, 0, 0)),
            pl.BlockSpec((None, 1, D), lambda i: (i // rows_per_b, 0, 0)),
            pl.BlockSpec((D, LANES), lambda i: (0, 0)),
            pl.BlockSpec((1, LANES), lambda i: (0, 0)),
        ],
        out_specs=pl.BlockSpec((tm, LANES), lambda i: (i, 0)),
        out_shape=jax.ShapeDtypeStruct((T, LANES), F32),
        scratch_shapes=[pltpu.VMEM((1, LANES), F32)],
        compiler_params=_params(("arbitrary",), 2 * tm * D * 4 + 4 * tm * D * 4),
        name="forget_cumsum",
    )(h, gain.reshape(1, D), scale, shift, w_pad, b_pad)


def _router_kernel(h_ref, gain_ref, scale_ref, shift_ref, w_ref, u_ref, route_ref, w0_ref, w1_ref,
                   count_ref, carry_ref, *, n_experts):
    @pl.when(pl.program_id(0) == 0)
    def _():
        carry_ref[...] = jnp.zeros_like(carry_ref)

    u, logits = _small_proj(h_ref, gain_ref, scale_ref, shift_ref, w_ref)
    u_ref[...] = u.astype(u_ref.dtype)
    tm = logits.shape[0]
    lane = lax.broadcasted_iota(jnp.int32, (tm, LANES), 1)
    neg = -jnp.inf
    l0 = jnp.where(lane < n_experts, logits, neg)
    m0 = jnp.max(l0, axis=-1, keepdims=True)
    i0 = jnp.min(jnp.where(l0 == m0, lane, LANES), axis=-1, keepdims=True)
    l1 = jnp.where(lane == i0, neg, l0)
    m1 = jnp.max(l1, axis=-1, keepdims=True)
    i1 = jnp.min(jnp.where(l1 == m1, lane, LANES), axis=-1, keepdims=True)
    e = jnp.exp(m1 - m0)
    wgt0 = 1.0 / (1.0 + e)
    wgt1 = e / (1.0 + e)
    onehot = jnp.where(lane == i0, 1.0, jnp.where(lane == i1, 1.0, 0.0))
    row = lax.broadcasted_iota(jnp.int32, (tm, tm), 0)
    col = lax.broadcasted_iota(jnp.int32, (tm, tm), 1)
    tri = jnp.where(col < row, 1.0, 0.0).astype(BF16)
    before = jnp.dot(tri, onehot.astype(BF16), preferred_element_type=F32) + carry_ref[...]
    rank0 = jnp.sum(jnp.where(lane == i0, before, 0.0), axis=-1, keepdims=True)
    rank1 = jnp.sum(jnp.where(lane == i1, before, 0.0), axis=-1, keepdims=True)
    total = carry_ref[...] + jnp.sum(onehot, axis=0, keepdims=True)
    carry_ref[...] = total
    count_ref[...] = jnp.broadcast_to(total, count_ref.shape)
    route = jnp.where(lane == 0, i0.astype(F32),
                      jnp.where(lane == 1, i1.astype(F32),
                                jnp.where(lane == 2, rank0, jnp.where(lane == 3, rank1, 0.0))))
    route_ref[...] = route
    w0_ref[...] = jnp.broadcast_to(wgt0, w0_ref.shape)
    w1_ref[...] = jnp.broadcast_to(wgt1, w1_ref.shape)


def _router(h, gain, scale, shift, w_router, seq, *, tm=512):
    T, D = h.shape
    E = w_router.shape[1]
    tm = _tile(seq, tm)
    rows_per_b = seq // tm
    w_pad = jnp.zeros((D, LANES), F32).at[:, :E].set(w_router)
    kern = functools.partial(_router_kernel, n_experts=E)
    row_spec = pl.BlockSpec((tm, LANES), lambda i: (i, 0))
    row_shape = jax.ShapeDtypeStruct((T, LANES), F32)
    return pl.pallas_call(
        kern,
        grid=(T // tm,),
        in_specs=[
            pl.BlockSpec((tm, D), lambda i: (i, 0)),
            pl.BlockSpec((1, D), lambda i: (0, 0)),
            pl.BlockSpec((None, 1, D), lambda i: (i // rows_per_b, 0, 0)),
            pl.BlockSpec((None, 1, D), lambda i: (i // rows_per_b, 0, 0)),
            pl.BlockSpec((D, LANES), lambda i: (0, 0)),
        ],
        out_specs=[
            pl.BlockSpec((tm, D), lambda i: (i, 0)),
            row_spec, row_spec, row_spec,
            pl.BlockSpec((8, LANES), lambda i: (0, 0)),
        ],
        out_shape=[
            jax.ShapeDtypeStruct((T, D), BF16),
            row_shape, row_shape, row_shape,
            jax.ShapeDtypeStruct((8, LANES), F32),
        ],
        scratch_shapes=[pltpu.VMEM((1, LANES), F32)],
        compiler_params=_params(("arbitrary",), 2 * tm * D * 4 + 2 * tm * D * 2 + 4 * tm * D * 4),
        name="moe_router",
    )(h, gain.reshape(1, D), scale, shift, w_pad)


def _fox_kernel(fref_ref, q_ref, k_ref, v_ref, fk_ref, o_ref, *, tq):
    bh = pl.program_id(0) * pl.num_programs(1) + pl.program_id(1)
    i = pl.program_id(2)
    q = q_ref[...]
    fref = fref_ref[bh, i]

    def step(j, carry, masked):
        m, l, acc = carry
        c0 = pl.multiple_of(j * tq, tq)
        k = k_ref[pl.ds(c0, tq), :]
        s = lax.dot_general(q, k, (((1,), (1,)), ((), ())), preferred_element_type=F32)
        s = s + (fref - fk_ref[:, pl.ds(c0, tq)])
        if masked:
            row = lax.broadcasted_iota(jnp.int32, (tq, tq), 0)
            col = lax.broadcasted_iota(jnp.int32, (tq, tq), 1)
            s = jnp.where(col <= row, s, -jnp.inf)
        m_new = jnp.maximum(m, jnp.max(s, axis=-1, keepdims=True))
        alpha = jnp.exp(m - m_new)
        p = jnp.exp(s - m_new)
        l = alpha * l + jnp.sum(p, axis=-1, keepdims=True)
        acc = alpha * acc + jnp.dot(p.astype(BF16), v_ref[pl.ds(c0, tq), :], preferred_element_type=F32)
        return m_new, l, acc

    init = (jnp.full((tq, 1), -jnp.inf, F32), jnp.zeros((tq, 1), F32), jnp.zeros((tq, FOX_DH), F32))
    carry = lax.fori_loop(0, i, functools.partial(step, masked=False), init)
    _, l, acc = step(i, carry, True)
    o_ref[...] = (acc / l).astype(o_ref.dtype)


def _fox_attention(q, kv, cum_rows, fref, batch, seq, heads, *, tq=512):
    T = q.shape[0]
    tq = _tile(seq, tq)
    nq = seq // tq
    kern = functools.partial(_fox_kernel, tq=tq)
    return pl.pallas_call(
        kern,
        grid=(batch, heads, nq),
        in_specs=[
            pl.BlockSpec(memory_space=pltpu.SMEM),
            pl.BlockSpec((tq, FOX_DH), lambda b, h, i: (b * nq + i, h)),
            pl.BlockSpec((seq, FOX_DH), lambda b, h, i: (b, h)),
            pl.BlockSpec((seq, FOX_DH), lambda b, h, i: (b, heads + h)),
            pl.BlockSpec((None, 1, seq), lambda b, h, i: (b * heads + h, 0, 0)),
        ],
        out_specs=pl.BlockSpec((tq, FOX_DH), lambda b, h, i: (b * nq + i, h)),
        out_shape=jax.ShapeDtypeStruct((T, heads * FOX_DH), BF16),
        compiler_params=_params(("arbitrary", "arbitrary", "arbitrary"),
                                4 * seq * FOX_DH * 2 + 8 * tq * tq * 4),
        name="fox_attention",
    )(fref, q, kv, kv, cum_rows)


def _swiglu_step(u, w1_ref, w3_ref, w2_ref, acc_ref):
    a = jnp.dot(u, w1_ref[...], preferred_element_type=F32)
    g = jnp.dot(u, w3_ref[...], preferred_element_type=F32)
    mid = (a * jax.nn.sigmoid(a) * g).astype(BF16)
    acc_ref[...] += jnp.dot(mid, w2_ref[...], preferred_element_type=F32)


def _dense_ffn_kernel(h_ref, gain_ref, scale_ref, shift_ref, gate_ref, w1_ref, w3_ref, w2_ref,
                      o_ref, u_ref, acc_ref):
    f = pl.program_id(1)

    @pl.when(f == 0)
    def _():
        u_ref[...] = _norm_mod(h_ref[...], gain_ref[...], scale_ref[...], shift_ref[...]).astype(BF16)
        acc_ref[...] = jnp.zeros_like(acc_ref)

    _swiglu_step(u_ref[...], w1_ref, w3_ref, w2_ref, acc_ref)

    @pl.when(f == pl.num_programs(1) - 1)
    def _():
        o_ref[...] = h_ref[...] + gate_ref[...] * acc_ref[...]


def _dense_ffn(h, gain, scale, shift, gate, w13, w2, seq, *, tm=512, tf=512):
    T, D = h.shape
    F = w2.shape[0]
    tm, tf = _tile(seq, tm), _tile(F, tf)
    nf = F // tf
    rows_per_b = seq // tm
    vec = pl.BlockSpec((None, 1, D), lambda i, f: (i // rows_per_b, 0, 0))
    vmem = 4 * tm * D * 4 + tm * D * 2 + tm * D * 4 + 2 * 3 * D * tf * 2
    return pl.pallas_call(
        _dense_ffn_kernel,
        grid=(T // tm, nf),
        in_specs=[
            pl.BlockSpec((tm, D), lambda i, f: (i, 0)),
            pl.BlockSpec((1, D), lambda i, f: (0, 0)),
            vec, vec, vec,
            pl.BlockSpec((D, tf), lambda i, f: (0, f)),
            pl.BlockSpec((D, tf), lambda i, f: (0, nf + f)),
            pl.BlockSpec((tf, D), lambda i, f: (f, 0)),
        ],
        out_specs=pl.BlockSpec((tm, D), lambda i, f: (i, 0)),
        out_shape=jax.ShapeDtypeStruct((T, D), F32),
        scratch_shapes=[pltpu.VMEM((tm, D), BF16), pltpu.VMEM((tm, D), F32)],
        compiler_params=_params(("arbitrary", "arbitrary"), vmem),
        name="dense_swiglu",
    )(h, gain.reshape(1, D), scale, shift, gate, w13, w13, w2)


def _expert_ffn_kernel(te_ref, nu_ref, x_ref, w1_ref, w3_ref, w2_ref, o_ref, acc_ref):
    i, f = pl.program_id(0), pl.program_id(1)
    used = i < nu_ref[0]

    @pl.when(f == 0)
    def _():
        acc_ref[...] = jnp.zeros_like(acc_ref)

    @pl.when(used)
    def _():
        _swiglu_step(x_ref[...], w1_ref, w3_ref, w2_ref, acc_ref)

    @pl.when(f == pl.num_programs(1) - 1)
    def _():
        o_ref[...] = acc_ref[...].astype(o_ref.dtype)


def _expert_ffn(xs, tile_expert, n_used, w13, w2, *, tm, tf=512):
    P, D = xs.shape
    E, F, _ = w2.shape
    tf = _tile(F, tf)
    nf = F // tf

    def fidx(i, f, nu):
        return jnp.where(i < nu[0], f, nf - 1)

    grid_spec = pltpu.PrefetchScalarGridSpec(
        num_scalar_prefetch=2,
        grid=(P // tm, nf),
        in_specs=[
            pl.BlockSpec((tm, D), lambda i, f, te, nu: (i, 0)),
            pl.BlockSpec((None, D, tf), lambda i, f, te, nu: (te[i], 0, fidx(i, f, nu))),
            pl.BlockSpec((None, D, tf), lambda i, f, te, nu: (te[i], 0, nf + fidx(i, f, nu))),
            pl.BlockSpec((None, tf, D), lambda i, f, te, nu: (te[i], fidx(i, f, nu), 0)),
        ],
        out_specs=pl.BlockSpec((tm, D), lambda i, f, te, nu: (i, 0)),
        scratch_shapes=[pltpu.VMEM((tm, D), F32)],
    )
    vmem = 2 * tm * D * 2 + 2 * tm * D * 2 + tm * D * 4 + 2 * 3 * D * tf * 2
    return pl.pallas_call(
        _expert_ffn_kernel,
        grid_spec=grid_spec,
        out_shape=jax.ShapeDtypeStruct((P, D), BF16),
        compiler_params=_params(("arbitrary", "arbitrary"), vmem),
        name="expert_swiglu",
    )(tile_expert, n_used, xs, w13, w13, w2)


def _dispatch_kernel(pos_ref, u_ref, xs_in_ref, xs_ref, sem, *, chunk):
    del xs_in_ref
    base = pl.program_id(0) * chunk

    def copy(r):
        return pltpu.make_async_copy(u_ref.at[base + r // TOP_K], xs_ref.at[pos_ref[r]], sem)

    def start(r, c):
        copy(r).start()
        return c

    def wait(r, c):
        copy(r).wait()
        return c

    lax.fori_loop(0, TOP_K * chunk, start, 0)
    lax.fori_loop(0, TOP_K * chunk, wait, 0)


def _dispatch(u3, pos_flat, n_slots, *, chunk=512):
    T = u3.shape[0]
    chunk = _tile(T, chunk)
    xs0 = jnp.zeros((n_slots,) + u3.shape[1:], u3.dtype)
    kern = functools.partial(_dispatch_kernel, chunk=chunk)
    return pl.pallas_call(
        kern,
        grid=(T // chunk,),
        in_specs=[
            pl.BlockSpec((TOP_K * chunk,), lambda i: (i,), memory_space=pltpu.SMEM),
            pl.BlockSpec(memory_space=pl.ANY),
            pl.BlockSpec(memory_space=pl.ANY),
        ],
        out_specs=pl.BlockSpec(memory_space=pl.ANY),
        out_shape=jax.ShapeDtypeStruct(xs0.shape, xs0.dtype),
        scratch_shapes=[pltpu.SemaphoreType.DMA(())],
        input_output_aliases={2: 0},
        compiler_params=_params(("arbitrary",), 1024 * 1024),
        name="moe_dispatch",
    )(pos_flat, u3, xs0)


def _combine_kernel(pos_ref, ys_ref, w0_ref, w1_ref, o_ref, buf_ref, sem, *, chunk):
    def copy(r):
        return pltpu.make_async_copy(ys_ref.at[pos_ref[r]], buf_ref.at[r % TOP_K, r // TOP_K], sem)

    def start(r, c):
        copy(r).start()
        return c

    def wait(r, c):
        copy(r).wait()
        return c

    lax.fori_loop(0, TOP_K * chunk, start, 0)
    lax.fori_loop(0, TOP_K * chunk, wait, 0)
    y = w0_ref[...] * buf_ref[0].astype(F32) + w1_ref[...] * buf_ref[1].astype(F32)
    o_ref[...] = y.astype(o_ref.dtype)


def _combine(ys3, pos_flat, w0, w1, n_tokens, *, chunk=512):
    tile = ys3.shape[1:]
    chunk = _tile(n_tokens, chunk)
    kern = functools.partial(_combine_kernel, chunk=chunk)
    wspec = pl.BlockSpec((chunk, 1, LANES), lambda i: (i, 0, 0))
    return pl.pallas_call(
        kern,
        grid=(n_tokens // chunk,),
        in_specs=[
            pl.BlockSpec((TOP_K * chunk,), lambda i: (i,), memory_space=pltpu.SMEM),
            pl.BlockSpec(memory_space=pl.ANY),
            wspec, wspec,
        ],
        out_specs=pl.BlockSpec((chunk,) + tile, lambda i: (i, 0, 0)),
        out_shape=jax.ShapeDtypeStruct((n_tokens,) + tile, BF16),
        scratch_shapes=[pltpu.VMEM((TOP_K, chunk) + tile, BF16), pltpu.SemaphoreType.DMA(())],
        compiler_params=_params(("arbitrary",), 8 * chunk * tile[0] * tile[1] * 4),
        name="moe_combine",
    )(pos_flat, ys3, w0.reshape(n_tokens, 1, LANES), w1.reshape(n_tokens, 1, LANES))


def _residual_kernel(h_ref, y_ref, gate_ref, o_ref):
    o_ref[...] = h_ref[...] + gate_ref[...] * y_ref[...].astype(F32)


def _gated_residual(h, y, gate, seq, *, tm=512):
    T, D = h.shape
    tm = _tile(seq, tm)
    rows_per_b = seq // tm
    return pl.pallas_call(
        _residual_kernel,
        grid=(T // tm,),
        in_specs=[
            pl.BlockSpec((tm, D), lambda i: (i, 0)),
            pl.BlockSpec((tm, D), lambda i: (i, 0)),
            pl.BlockSpec((None, 1, D), lambda i: (i // rows_per_b, 0, 0)),
        ],
        out_specs=pl.BlockSpec((tm, D), lambda i: (i, 0)),
        out_shape=jax.ShapeDtypeStruct((T, D), F32),
        compiler_params=_params(("arbitrary",), 6 * tm * D * 4),
        name="gated_residual",
    )(h, y, gate)


def _moe_layer(h, gain, scale, shift, gate, w_router, w13, w2, seq, *, tm=512):
    T, D = h.shape
    E = w_router.shape[1]
    u, route, w0, w1, counts = _router(h, gain, scale, shift, w_router, seq)

    expert = route[:, :TOP_K].astype(jnp.int32)
    rank = route[:, TOP_K:2 * TOP_K].astype(jnp.int32)
    count = counts[0, :E].astype(jnp.int32)
    padded = ((count + tm - 1) // tm) * tm
    ends = jnp.cumsum(padded)
    starts = ends - padded
    pos_flat = (starts[expert] + rank).reshape(-1)
    n_tiles = (TOP_K * T) // tm + E
    tile_start = jnp.arange(n_tiles, dtype=jnp.int32) * tm
    n_used = (ends[-1] // tm).astype(jnp.int32).reshape(1)
    tile_expert = jnp.minimum(jnp.searchsorted(ends, tile_start, side="right"), E - 1).astype(jnp.int32)
    tile_expert = jnp.where(tile_start < ends[-1], tile_expert, tile_expert[jnp.maximum(n_used[0] - 1, 0)])

    row_tile = (D // LANES, LANES)
    assert row_tile[0] % SUBLANES_BF16 == 0
    xs3 = _dispatch(u.reshape((T,) + row_tile), pos_flat, n_tiles * tm)
    ys = _expert_ffn(xs3.reshape(n_tiles * tm, D), tile_expert, n_used, w13, w2, tm=tm)
    y3 = _combine(ys.reshape((n_tiles * tm,) + row_tile), pos_flat, w0, w1, T)
    return _gated_residual(h, y3.reshape(T, D), gate, seq)


def kernel(x, c, norm_gain, mod_w, mod_b, ret_w_in, ret_w_out, fox_w_q, fox_q_gain, fox_w_out,
           kv_norm_gain, kv_mod_w, kv_mod_b, kv_w, kv_forget_bias, kv_k_gain,
           ffn_w13, ffn_w2, moe_router, moe_w13, moe_w2):
    B, S, D = x.shape
    T = B * S
    depth = mod_w.shape[0]
    n_ret = ret_w_in.shape[0]
    ret_heads = ret_w_out.shape[1] // RET_DV
    fox_heads = fox_w_q.shape[2] // FOX_DH
    fox_width = fox_heads * FOX_DH

    c8 = jnp.zeros((8, D), F32).at[:B].set(c)
    mod = _mod_call(c8, mod_w, mod_b)[:, :B]
    kv_mod = _mod_call(c8, kv_mod_w[None], kv_mod_b[None])[0, :B]

    def vecs(m, n):
        return [v.reshape(B, 1, D) for v in jnp.split(m, n, axis=-1)]

    h = x.reshape(T, D)
    kv_sh = cum_rows = fref = None
    fox_tq = _tile(S, 512)
    for i in range(depth):
        sh1, sc1, g1, sh2, sc2, g2 = vecs(mod[i], 6)
        if i < n_ret:
            proj = _nm_matmul(h, norm_gain[i, 0], sc1, sh1, ret_w_in[i].astype(BF16), S)
            y = _retention(proj, B, S, ret_heads)
            h = _proj_residual(y, ret_w_out[i].astype(BF16), h, g1, S)
        else:
            j = i - n_ret
            q = _nm_matmul(h, norm_gain[i, 0], sc1, sh1, fox_w_q[j].astype(BF16), S,
                           head_gain=fox_q_gain[j], norm_cols=fox_width, head_scale=FOX_DH ** -0.5)
            y = _fox_attention(q, kv_sh, cum_rows, fref, B, S, fox_heads, tq=fox_tq)
            h = _proj_residual(y, fox_w_out[j].astype(BF16), h, g1, S, tn=1024)
        if i % 2 == 0:
            h = _dense_ffn(h, norm_gain[i, 1], sc2, sh2, g2,
                           ffn_w13[i // 2].astype(BF16), ffn_w2[i // 2].astype(BF16), S)
        else:
            h = _moe_layer(h, norm_gain[i, 1], sc2, sh2, g2, moe_router[i // 2],
                           moe_w13[i // 2].astype(BF16), moe_w2[i // 2].astype(BF16), S)
        if i == n_ret - 1:
            ksh, ksc = vecs(kv_mod, 2)
            kv_sh = _nm_matmul(h, kv_norm_gain, ksc, ksh, kv_w[:, :2 * fox_width].astype(BF16), S,
                               head_gain=kv_k_gain, norm_cols=fox_width)
            cum = _forget_cumsum(h, kv_norm_gain, ksc, ksh, kv_w[:, 2 * fox_width:], kv_forget_bias, S)
            cum_bhs = cum[:, :fox_heads].reshape(B, S, fox_heads).transpose(0, 2, 1)
            cum_rows = cum_bhs.reshape(B * fox_heads, 1, S)
            fref = cum_bhs.reshape(B * fox_heads, S // fox_tq, fox_tq)[:, :, 0]
    return h.reshape(B, S, D)
```

```python
import functools
import math

import jax
import jax.numpy as jnp
from jax import lax
from jax.experimental import pallas as pl
from jax.experimental.pallas import tpu as pltpu

F32 = jnp.float32
BF16 = jnp.bfloat16
HIGHEST = lax.Precision.HIGHEST

EPS = 1e-6
ROPE_BASE = 10000.0
RET_DK = 256
RET_DV = 2 * RET_DK
FOX_DH = 128
TOP_K = 2
LOG2E = math.log2(math.e)
LANES = 128
SUBLANES_BF16 = 16
VMEM_LIMIT_CAP = 56 * 1024 * 1024


def _params(semantics, vmem_bytes):
    limit = int(min(max(2 * vmem_bytes, 16 * 1024 * 1024), VMEM_LIMIT_CAP))
    return pltpu.CompilerParams(dimension_semantics=semantics, vmem_limit_bytes=limit)


def _tile(n, pref):
    t = min(n, pref)
    assert n % t == 0, (n, pref)
    return t


def _norm_mod(h, gain, scale, shift):
    ms = jnp.mean(h * h, axis=-1, keepdims=True)
    return h * lax.rsqrt(ms + EPS) * gain * (1.0 + scale) + shift


def _mod_kernel(c_ref, w_ref, b_ref, o_ref):
    c = c_ref[...]
    cond = c * jax.nn.sigmoid(c)
    o_ref[...] = jnp.dot(cond, w_ref[...], precision=HIGHEST, preferred_element_type=F32) + b_ref[...]


def _mod_call(c8, w, b):
    L, D, N = w.shape
    tn = _tile(N, 1024)
    return pl.pallas_call(
        _mod_kernel,
        grid=(L, N // tn),
        in_specs=[
            pl.BlockSpec((8, D), lambda l, j: (0, 0)),
            pl.BlockSpec((None, D, tn), lambda l, j: (l, 0, j)),
            pl.BlockSpec((None, 1, tn), lambda l, j: (l, 0, j)),
        ],
        out_specs=pl.BlockSpec((None, 8, tn), lambda l, j: (l, 0, j)),
        out_shape=jax.ShapeDtypeStruct((L, 8, N), F32),
        compiler_params=_params(("arbitrary", "arbitrary"), 2 * D * tn * 4),
        name="adaln_mod",
    )(c8, w, b.reshape(L, 1, N))


def _nm_matmul_kernel(h_ref, gain_ref, scale_ref, shift_ref, w_ref, hg_ref, o_ref, u_ref,
                      *, n_norm_tiles, head_scale):
    j = pl.program_id(1)

    @pl.when(j == 0)
    def _():
        u_ref[...] = _norm_mod(h_ref[...], gain_ref[...], scale_ref[...], shift_ref[...]).astype(BF16)

    acc = jnp.dot(u_ref[...], w_ref[...], preferred_element_type=F32)

    if n_norm_tiles == 0:
        o_ref[...] = acc.astype(o_ref.dtype)
    else:
        @pl.when(j < n_norm_tiles)
        def _():
            hg = hg_ref[...] * head_scale
            for c in range(acc.shape[1] // FOX_DH):
                blk = acc[:, c * FOX_DH:(c + 1) * FOX_DH]
                ms = jnp.mean(blk * blk, axis=-1, keepdims=True)
                o_ref[:, c * FOX_DH:(c + 1) * FOX_DH] = (blk * lax.rsqrt(ms + EPS) * hg).astype(o_ref.dtype)

        @pl.when(j >= n_norm_tiles)
        def _():
            o_ref[...] = acc.astype(o_ref.dtype)


def _nm_matmul(h, gain, scale, shift, w, seq, *, head_gain=None, norm_cols=0, head_scale=1.0,
               tm=1024, tn=1024):
    T, D = h.shape
    N = w.shape[1]
    tm, tn = _tile(seq, tm), _tile(N, tn)
    assert norm_cols % tn == 0
    if head_gain is None:
        head_gain = jnp.ones((FOX_DH,), F32)
    rows_per_b = seq // tm
    vmem = 2 * tm * D * 4 + 2 * D * tn * 2 + 2 * tm * tn * 2 + tm * D * 2
    kern = functools.partial(_nm_matmul_kernel, n_norm_tiles=norm_cols // tn, head_scale=head_scale)
    return pl.pallas_call(
        kern,
        grid=(T // tm, N // tn),
        in_specs=[
            pl.BlockSpec((tm, D), lambda i, j: (i, 0)),
            pl.BlockSpec((1, D), lambda i, j: (0, 0)),
            pl.BlockSpec((None, 1, D), lambda i, j: (i // rows_per_b, 0, 0)),
            pl.BlockSpec((None, 1, D), lambda i, j: (i // rows_per_b, 0, 0)),
            pl.BlockSpec((D, tn), lambda i, j: (0, j)),
            pl.BlockSpec((1, FOX_DH), lambda i, j: (0, 0)),
        ],
        out_specs=pl.BlockSpec((tm, tn), lambda i, j: (i, j)),
        out_shape=jax.ShapeDtypeStruct((T, N), BF16),
        scratch_shapes=[pltpu.VMEM((tm, D), BF16)],
        compiler_params=_params(("arbitrary", "arbitrary"), vmem),
        name="norm_mod_proj",
    )(h, gain.reshape(1, D), scale, shift, w, head_gain.reshape(1, FOX_DH))


def _proj_residual_kernel(y_ref, w_ref, h_ref, gate_ref, o_ref):
    acc = jnp.dot(y_ref[...], w_ref[...], preferred_element_type=F32)
    o_ref[...] = h_ref[...] + gate_ref[...] * acc


def _proj_residual(y, w, h, gate, seq, *, tm=1024, tn=512):
    T, K = y.shape
    D = w.shape[1]
    tm, tn = _tile(seq, tm), _tile(D, tn)
    rows_per_b = seq // tm
    vmem = 2 * tm * K * 2 + 2 * K * tn * 2 + 4 * tm * tn * 4
    return pl.pallas_call(
        _proj_residual_kernel,
        grid=(T // tm, D // tn),
        in_specs=[
            pl.BlockSpec((tm, K), lambda i, j: (i, 0)),
            pl.BlockSpec((K, tn), lambda i, j: (0, j)),
            pl.BlockSpec((tm, tn), lambda i, j: (i, j)),
            pl.BlockSpec((None, 1, tn), lambda i, j: (i // rows_per_b, 0, j)),
        ],
        out_specs=pl.BlockSpec((tm, tn), lambda i, j: (i, j)),
        out_shape=jax.ShapeDtypeStruct((T, D), F32),
        compiler_params=_params(("arbitrary", "arbitrary"), vmem),
        name="proj_residual",
    )(y, w, h, gate)


def _rotary(x, cos, sin):
    half = x.shape[-1] // 2
    x1, x2 = x[:, :half], x[:, half:]
    return jnp.concatenate([x1 * cos - x2 * sin, x1 * sin + x2 * cos], axis=-1)


def _retention_kernel(lg_ref, q_ref, k_ref, v_ref, g_ref, cos_ref, sin_ref, o_ref, state_ref,
                      *, chunk, n_chunks):
    @pl.when(pl.program_id(2) == 0)
    def _():
        state_ref[...] = jnp.zeros_like(state_ref)

    lg = lg_ref[pl.program_id(1)]
    row = lax.broadcasted_iota(jnp.int32, (chunk, chunk), 0)
    col = lax.broadcasted_iota(jnp.int32, (chunk, chunk), 1)
    dist = (row - col).astype(F32)
    d_local = jnp.where(dist >= 0, jnp.exp(lg * jnp.maximum(dist, 0.0)), 0.0)
    t = lax.broadcasted_iota(jnp.int32, (chunk, 1), 0).astype(F32)
    xi = jnp.exp(lg * (t + 1.0))
    zeta = jnp.exp(lg * (chunk - 1.0 - t))
    gamma_c = jnp.exp(jnp.full((1, 1), lg * chunk, F32))

    def body(ci, carry):
        r0 = pl.multiple_of(ci * chunk, chunk)
        cos = cos_ref[pl.ds(r0, chunk), :]
        sin = sin_ref[pl.ds(r0, chunk), :]
        q = _rotary(q_ref[pl.ds(r0, chunk), :].astype(F32), cos, sin).astype(BF16)
        k = (_rotary(k_ref[pl.ds(r0, chunk), :].astype(F32), cos, sin) * (RET_DK ** -0.5)).astype(BF16)
        v = v_ref[pl.ds(r0, chunk), :]
        scores = lax.dot_general(q, k, (((1,), (1,)), ((), ())), preferred_element_type=F32) * d_local
        inner = jnp.dot(scores.astype(BF16), v, preferred_element_type=F32)
        state = state_ref[...]
        cross = jnp.dot(q, state.astype(BF16), preferred_element_type=F32) * xi
        vz = (v.astype(F32) * zeta).astype(BF16)
        state_ref[...] = gamma_c * state + lax.dot_general(
            k, vz, (((0,), (0,)), ((), ())), preferred_element_type=F32)
        o = inner + cross
        ms = jnp.mean(o * o, axis=-1, keepdims=True)
        g = g_ref[pl.ds(r0, chunk), :].astype(F32)
        o_ref[pl.ds(r0, chunk), :] = (g * jax.nn.sigmoid(g) * (o * lax.rsqrt(ms + EPS))).astype(o_ref.dtype)
        return carry

    lax.fori_loop(0, n_chunks, body, 0)


def _retention(proj, batch, seq, heads, *, chunk=256, rows=1024):
    T = proj.shape[0]
    chunk = _tile(seq, chunk)
    rows = _tile(seq, rows)
    nblk = seq // rows
    half = RET_DK // 2
    inv = 1.0 / (ROPE_BASE ** (jnp.arange(half, dtype=F32) / half))
    ang = jnp.arange(seq, dtype=F32)[:, None] * inv[None, :]
    cos, sin = jnp.cos(ang), jnp.sin(ang)
    log_gamma = jnp.log1p(-jnp.exp2(-5.0 - jnp.arange(heads, dtype=F32)))
    kern = functools.partial(_retention_kernel, chunk=chunk, n_chunks=rows // chunk)
    qk_blocks = heads
    v_off = 2 * heads * RET_DK // RET_DV
    return pl.pallas_call(
        kern,
        grid=(batch, heads, nblk),
        in_specs=[
            pl.BlockSpec(memory_space=pltpu.SMEM),
            pl.BlockSpec((rows, RET_DK), lambda b, h, c: (b * nblk + c, h)),
            pl.BlockSpec((rows, RET_DK), lambda b, h, c: (b * nblk + c, qk_blocks + h)),
            pl.BlockSpec((rows, RET_DV), lambda b, h, c: (b * nblk + c, v_off + h)),
            pl.BlockSpec((rows, RET_DV), lambda b, h, c: (b * nblk + c, v_off + heads + h)),
            pl.BlockSpec((rows, half), lambda b, h, c: (c, 0)),
            pl.BlockSpec((rows, half), lambda b, h, c: (c, 0)),
        ],
        out_specs=pl.BlockSpec((rows, RET_DV), lambda b, h, c: (b * nblk + c, h)),
        out_shape=jax.ShapeDtypeStruct((T, heads * RET_DV), BF16),
        scratch_shapes=[pltpu.VMEM((RET_DK, RET_DV), F32)],
        compiler_params=_params(("arbitrary", "arbitrary", "arbitrary"), 16 * 1024 * 1024),
        name="retention",
    )(log_gamma, proj, proj, proj, proj, cos, sin)


def _small_proj(h_ref, gain_ref, scale_ref, shift_ref, w_ref):
    u = _norm_mod(h_ref[...], gain_ref[...], scale_ref[...], shift_ref[...])
    return u, jnp.dot(u, w_ref[...], precision=HIGHEST, preferred_element_type=F32)


def _forget_kernel(h_ref, gain_ref, scale_ref, shift_ref, w_ref, bias_ref, o_ref, carry_ref,
                   *, tiles_per_seq, n_heads):
    @pl.when(pl.program_id(0) % tiles_per_seq == 0)
    def _():
        carry_ref[...] = jnp.zeros_like(carry_ref)

    _, f = _small_proj(h_ref, gain_ref, scale_ref, shift_ref, w_ref)
    log_f = jax.nn.log_sigmoid(f + bias_ref[...])
    tm = log_f.shape[0]
    row = lax.broadcasted_iota(jnp.int32, (tm, tm), 0)
    col = lax.broadcasted_iota(jnp.int32, (tm, tm), 1)
    tri = jnp.where(col <= row, 1.0, 0.0).astype(F32)
    cum = jnp.dot(tri, log_f, precision=HIGHEST, preferred_element_type=F32) + carry_ref[...]
    carry_ref[...] = cum[tm - 1:tm, :]
    neg = cum * (-LOG2E)
    lane = lax.broadcasted_iota(jnp.int32, (tm, FOX_DH), 1)
    for hh in range(n_heads):
        b = neg[:, hh:hh + 1]
        hi = b.astype(BF16).astype(F32)
        mid = (b - hi).astype(BF16).astype(F32)
        lo = b - hi - mid
        blk = jnp.where(lane == 0, hi, jnp.where(lane == 1, mid, jnp.where(lane == 2, lo, 0.0)))
        o_ref[:, hh * FOX_DH:(hh + 1) * FOX_DH] = blk.astype(o_ref.dtype)


def _forget_cumsum(h, gain, scale, shift, w_f, bias, seq, *, tm=512):
    T, D = h.shape
    nh = w_f.shape[1]
    tm = _tile(seq, tm)
    rows_per_b = seq // tm
    w_pad = jnp.zeros((D, LANES), F32).at[:, :nh].set(w_f)
    b_pad = jnp.zeros((1, LANES), F32).at[0, :nh].set(bias)
    kern = functools.partial(_forget_kernel, tiles_per_seq=rows_per_b, n_heads=nh)
    return pl.pallas_call(
        kern,
        grid=(T // tm,),
        in_specs=[
            pl.BlockSpec((tm, D), lambda i: (i, 0)),
            pl.BlockSpec((1, D), lambda i: (0, 0)),
            pl.BlockSpec((None, 1, D), lambda i: (i // rows_per_b, 0, 0)),
            pl.BlockSpec((None, 1, D), lambda i: (i // rows_per_b, 0, 0)),
            pl.BlockSpec((D, LANES), lambda i: (0, 0)),
            pl.BlockSpec((1, LANES), lambda i: (0, 0)),
        ],
        out_specs=pl.BlockSpec((tm, nh * FOX_DH), lambda i: (i, 0)),
        out_shape=jax.ShapeDtypeStruct((T, nh * FOX_DH), BF16),
        scratch_shapes=[pltpu.VMEM((1, LANES), F32)],
        compiler_params=_params(("arbitrary",), 2 * tm * D * 4 + 4 * tm * D * 4),
        name="forget_cumsum",
    )(h, gain.reshape(1, D), scale, shift, w_pad, b_pad)


def _router_kernel(h_ref, gain_ref, scale_ref, shift_ref, w_ref, u_ref, route_ref, w0_ref, w1_ref,
                   count_ref, carry_ref, *, n_experts):
    @pl.when(pl.program_id(0) == 0)
    def _():
        carry_ref[...] = jnp.zeros_like(carry_ref)

    u, logits = _small_proj(h_ref, gain_ref, scale_ref, shift_ref, w_ref)
    u_ref[...] = u.astype(u_ref.dtype)
    tm = logits.shape[0]
    lane = lax.broadcasted_iota(jnp.int32, (tm, LANES), 1)
    neg = -jnp.inf
    l0 = jnp.where(lane < n_experts, logits, neg)
    m0 = jnp.max(l0, axis=-1, keepdims=True)
    i0 = jnp.min(jnp.where(l0 == m0, lane, LANES), axis=-1, keepdims=True)
    l1 = jnp.where(lane == i0, neg, l0)
    m1 = jnp.max(l1, axis=-1, keepdims=True)
    i1 = jnp.min(jnp.where(l1 == m1, lane, LANES), axis=-1, keepdims=True)
    e = jnp.exp(m1 - m0)
    wgt0 = 1.0 / (1.0 + e)
    wgt1 = e / (1.0 + e)
    onehot = jnp.where(lane == i0, 1.0, jnp.where(lane == i1, 1.0, 0.0))
    row = lax.broadcasted_iota(jnp.int32, (tm, tm), 0)
    col = lax.broadcasted_iota(jnp.int32, (tm, tm), 1)
    tri = jnp.where(col < row, 1.0, 0.0).astype(BF16)
    before = jnp.dot(tri, onehot.astype(BF16), preferred_element_type=F32) + carry_ref[...]
    rank0 = jnp.sum(jnp.where(lane == i0, before, 0.0), axis=-1, keepdims=True)
    rank1 = jnp.sum(jnp.where(lane == i1, before, 0.0), axis=-1, keepdims=True)
    total = carry_ref[...] + jnp.sum(onehot, axis=0, keepdims=True)
    carry_ref[...] = total
    count_ref[...] = jnp.broadcast_to(total, count_ref.shape)
    route = jnp.where(lane == 0, i0.astype(F32),
                      jnp.where(lane == 1, i1.astype(F32),
                                jnp.where(lane == 2, rank0, jnp.where(lane == 3, rank1, 0.0))))
    route_ref[...] = route
    w0_ref[...] = jnp.broadcast_to(wgt0, w0_ref.shape)
    w1_ref[...] = jnp.broadcast_to(wgt1, w1_ref.shape)


def _router(h, gain, scale, shift, w_router, seq, *, tm=512):
    T, D = h.shape
    E = w_router.shape[1]
    tm = _tile(seq, tm)
    rows_per_b = seq // tm
    w_pad = jnp.zeros((D, LANES), F32).at[:, :E].set(w_router)
    kern = functools.partial(_router_kernel, n_experts=E)
    row_spec = pl.BlockSpec((tm, LANES), lambda i: (i, 0))
    row_shape = jax.ShapeDtypeStruct((T, LANES), F32)
    return pl.pallas_call(
        kern,
        grid=(T // tm,),
        in_specs=[
            pl.BlockSpec((tm, D), lambda i: (i, 0)),
            pl.BlockSpec((1, D), lambda i: (0, 0)),
            pl.BlockSpec((None, 1, D), lambda i: (i // rows_per_b, 0, 0)),
            pl.BlockSpec((None, 1, D), lambda i: (i // rows_per_b, 0, 0)),
            pl.BlockSpec((D, LANES), lambda i: (0, 0)),
        ],
        out_specs=[
            pl.BlockSpec((tm, D), lambda i: (i, 0)),
            row_spec, row_spec, row_spec,
            pl.BlockSpec((8, LANES), lambda i: (0, 0)),
        ],
        out_shape=[
            jax.ShapeDtypeStruct((T, D), BF16),
            row_shape, row_shape, row_shape,
            jax.ShapeDtypeStruct((8, LANES), F32),
        ],
        scratch_shapes=[pltpu.VMEM((1, LANES), F32)],
        compiler_params=_params(("arbitrary",), 2 * tm * D * 4 + 2 * tm * D * 2 + 4 * tm * D * 4),
        name="moe_router",
    )(h, gain.reshape(1, D), scale, shift, w_pad)


def _fox_kernel(ti_ref, tj_ref, q_ref, k_ref, e_ref, v_ref, o_ref, kp_ref, vp_ref, s0_ref, s1_ref,
                p0_ref, p1_ref, a0_ref, a1_ref, m_ref, acc_ref, *, tq, n_off, n_diag):
    lane = lax.broadcasted_iota(jnp.int32, k_ref.shape, 1)
    kp_ref[:, :FOX_DH] = k_ref[...]
    kp_ref[:, FOX_DH:] = e_ref[...]
    vp_ref[:, :FOX_DH] = v_ref[...]
    vp_ref[:, FOX_DH:] = jnp.where(lane == 0, 1.0, 0.0).astype(BF16)
    m_ref[...] = jnp.full(m_ref.shape, -jnp.inf, F32)
    acc_ref[...] = jnp.zeros_like(acc_ref)

    STEPS_PER_GROUP = 2
    lane = lax.broadcasted_iota(jnp.int32, (tq, FOX_DH), 1)
    ones = jnp.where(lane < 3, 1.0, 0.0).astype(BF16)
    s_refs, p_refs, a_refs = (s0_ref, s1_ref), (p0_ref, p1_ref), (a0_ref, a1_ref)

    def tile_rows(idx_ref, t):
        return pl.ds(pl.multiple_of(idx_ref[t] * tq, tq), tq)

    def logits(t, par):
        qp = jnp.concatenate([q_ref[tile_rows(ti_ref, t), :], ones], axis=-1)
        kp = kp_ref[tile_rows(tj_ref, t), :]
        s_refs[par][...] = lax.dot_general(qp, kp, (((1,), (1,)), ((), ())), preferred_element_type=F32)

    def softmax(t, par, diag):
        rows = tile_rows(ti_ref, t)
        s = s_refs[par][...]
        if diag:
            row = lax.broadcasted_iota(jnp.int32, (tq, tq), 0)
            col = lax.broadcasted_iota(jnp.int32, (tq, tq), 1)
            s = jnp.where(col <= row, s, -jnp.inf)
        m = m_ref[rows, :]
        m_new = jnp.maximum(m, jnp.max(s, axis=-1, keepdims=True))
        m_ref[rows, :] = m_new
        a_refs[par][...] = jnp.exp2(m - m_new)
        p_refs[par][...] = jnp.exp2(s - m_new).astype(BF16)

    def values(t, par, diag):
        rows = tile_rows(ti_ref, t)
        vp = vp_ref[tile_rows(tj_ref, t), :]
        acc = a_refs[par][...] * acc_ref[rows, :] + jnp.dot(p_refs[par][...], vp,
                                                            preferred_element_type=F32)
        if diag:
            o_ref[rows, :] = (acc[:, :FOX_DH] / acc[:, FOX_DH:FOX_DH + 1]).astype(o_ref.dtype)
        else:
            acc_ref[rows, :] = acc

    def pipeline(t0, n, diag):
        def step(u, par):
            static = isinstance(u, int)
            if not static or 2 <= u <= n + 1:
                values(t0 + u - 2, par, diag)
            if not static or 1 <= u <= n:
                softmax(t0 + u - 1, 1 - par, diag)
            if not static or u <= n - 1:
                logits(t0 + u, par)

        def group(d, c):
            for g in range(STEPS_PER_GROUP):
                step(2 + STEPS_PER_GROUP * d + g, g % 2)
            return c

        head = min(2, n + 2)
        n_groups = max(n - 2, 0) // STEPS_PER_GROUP
        for u in range(head):
            step(u, u % 2)
        lax.fori_loop(0, n_groups, group, 0)
        for u in range(head + STEPS_PER_GROUP * n_groups, n + 2):
            step(u, u % 2)

    pipeline(0, n_off, False)
    pipeline(n_off, n_diag, True)


def _fox_attention(q, kv, e, batch, seq, heads, *, tq=512):
    T = q.shape[0]
    tq = _tile(seq, tq)
    nq = seq // tq
    below = [(i, j) for i in range(nq) for j in range(i)]
    tiles = below + [(i, i) for i in range(nq)]
    ti = jnp.array([t[0] for t in tiles], jnp.int32)
    tj = jnp.array([t[1] for t in tiles], jnp.int32)
    kern = functools.partial(_fox_kernel, tq=tq, n_off=len(below), n_diag=nq)
    grid_spec = pltpu.PrefetchScalarGridSpec(
        num_scalar_prefetch=2,
        grid=(batch, heads),
        in_specs=[
            pl.BlockSpec((seq, FOX_DH), lambda b, h, ti, tj: (b, h)),
            pl.BlockSpec((seq, FOX_DH), lambda b, h, ti, tj: (b, h)),
            pl.BlockSpec((seq, FOX_DH), lambda b, h, ti, tj: (b, h)),
            pl.BlockSpec((seq, FOX_DH), lambda b, h, ti, tj: (b, heads + h)),
        ],
        out_specs=pl.BlockSpec((seq, FOX_DH), lambda b, h, ti, tj: (b, h)),
        scratch_shapes=[pltpu.VMEM((seq, 2 * FOX_DH), BF16), pltpu.VMEM((seq, 2 * FOX_DH), BF16),
                        pltpu.VMEM((tq, tq), F32), pltpu.VMEM((tq, tq), F32),
                        pltpu.VMEM((tq, tq), BF16), pltpu.VMEM((tq, tq), BF16),
                        pltpu.VMEM((tq, 1), F32), pltpu.VMEM((tq, 1), F32), pltpu.VMEM((seq, 1), F32),
                        pltpu.VMEM((seq, 2 * FOX_DH), F32)],
    )
    vmem = 14 * seq * FOX_DH * 2 + seq * (LANES + 2 * FOX_DH) * 4 + 8 * tq * tq * 4
    return pl.pallas_call(
        kern,
        grid_spec=grid_spec,
        out_shape=jax.ShapeDtypeStruct((T, heads * FOX_DH), BF16),
        compiler_params=_params(("arbitrary", "arbitrary"), vmem),
        name="fox_attention",
    )(ti, tj, q, kv, e, kv)


def _swiglu_step(u, w1_ref, w3_ref, w2_ref, acc_ref):
    a = jnp.dot(u, w1_ref[...], preferred_element_type=F32)
    g = jnp.dot(u, w3_ref[...], preferred_element_type=F32)
    mid = (a * jax.nn.sigmoid(a) * g).astype(BF16)
    acc_ref[...] += jnp.dot(mid, w2_ref[...], preferred_element_type=F32)


def _dense_ffn_kernel(h_ref, gain_ref, scale_ref, shift_ref, gate_ref, w1_ref, w3_ref, w2_ref,
                      o_ref, u_ref, acc_ref):
    f = pl.program_id(1)

    @pl.when(f == 0)
    def _():
        u_ref[...] = _norm_mod(h_ref[...], gain_ref[...], scale_ref[...], shift_ref[...]).astype(BF16)
        acc_ref[...] = jnp.zeros_like(acc_ref)

    _swiglu_step(u_ref[...], w1_ref, w3_ref, w2_ref, acc_ref)

    @pl.when(f == pl.num_programs(1) - 1)
    def _():
        o_ref[...] = h_ref[...] + gate_ref[...] * acc_ref[...]


def _dense_ffn(h, gain, scale, shift, gate, w13, w2, seq, *, tm=512, tf=512):
    T, D = h.shape
    F = w2.shape[0]
    tm, tf = _tile(seq, tm), _tile(F, tf)
    nf = F // tf
    rows_per_b = seq // tm
    vec = pl.BlockSpec((None, 1, D), lambda i, f: (i // rows_per_b, 0, 0))
    vmem = 4 * tm * D * 4 + tm * D * 2 + tm * D * 4 + 2 * 3 * D * tf * 2
    return pl.pallas_call(
        _dense_ffn_kernel,
        grid=(T // tm, nf),
        in_specs=[
            pl.BlockSpec((tm, D), lambda i, f: (i, 0)),
            pl.BlockSpec((1, D), lambda i, f: (0, 0)),
            vec, vec, vec,
            pl.BlockSpec((D, tf), lambda i, f: (0, f)),
            pl.BlockSpec((D, tf), lambda i, f: (0, nf + f)),
            pl.BlockSpec((tf, D), lambda i, f: (f, 0)),
        ],
        out_specs=pl.BlockSpec((tm, D), lambda i, f: (i, 0)),
        out_shape=jax.ShapeDtypeStruct((T, D), F32),
        scratch_shapes=[pltpu.VMEM((tm, D), BF16), pltpu.VMEM((tm, D), F32)],
        compiler_params=_params(("arbitrary", "arbitrary"), vmem),
        name="dense_swiglu",
    )(h, gain.reshape(1, D), scale, shift, gate, w13, w13, w2)


def _expert_ffn_kernel(te_ref, nu_ref, x_ref, w1_ref, w3_ref, w2_ref, o_ref, acc_ref):
    i, f = pl.program_id(0), pl.program_id(1)
    used = i < nu_ref[0]

    @pl.when(f == 0)
    def _():
        acc_ref[...] = jnp.zeros_like(acc_ref)

    @pl.when(used)
    def _():
        _swiglu_step(x_ref[...], w1_ref, w3_ref, w2_ref, acc_ref)

    @pl.when(f == pl.num_programs(1) - 1)
    def _():
        o_ref[...] = acc_ref[...].astype(o_ref.dtype)


def _expert_ffn(xs, tile_expert, n_used, w13, w2, *, tm, tf=512):
    P, D = xs.shape
    E, F, _ = w2.shape
    tf = _tile(F, tf)
    nf = F // tf

    def fidx(i, f, nu):
        return jnp.where(i < nu[0], f, nf - 1)

    grid_spec = pltpu.PrefetchScalarGridSpec(
        num_scalar_prefetch=2,
        grid=(P // tm, nf),
        in_specs=[
            pl.BlockSpec((tm, D), lambda i, f, te, nu: (i, 0)),
            pl.BlockSpec((None, D, tf), lambda i, f, te, nu: (te[i], 0, fidx(i, f, nu))),
            pl.BlockSpec((None, D, tf), lambda i, f, te, nu: (te[i], 0, nf + fidx(i, f, nu))),
            pl.BlockSpec((None, tf, D), lambda i, f, te, nu: (te[i], fidx(i, f, nu), 0)),
        ],
        out_specs=pl.BlockSpec((tm, D), lambda i, f, te, nu: (i, 0)),
        scratch_shapes=[pltpu.VMEM((tm, D), F32)],
    )
    vmem = 2 * tm * D * 2 + 2 * tm * D * 2 + tm * D * 4 + 2 * 3 * D * tf * 2
    return pl.pallas_call(
        _expert_ffn_kernel,
        grid_spec=grid_spec,
        out_shape=jax.ShapeDtypeStruct((P, D), BF16),
        compiler_params=_params(("arbitrary", "arbitrary"), vmem),
        name="expert_swiglu",
    )(tile_expert, n_used, xs, w13, w13, w2)


def _dispatch_kernel(pos_ref, u_ref, xs_in_ref, xs_ref, sem, *, chunk):
    del xs_in_ref

    def copy(r):
        return pltpu.make_async_copy(u_ref.at[r // TOP_K], xs_ref.at[pos_ref[r]], sem)

    def start(r, c):
        copy(r).start()
        return c

    def wait(r, c):
        copy(r).wait()
        return c

    lax.fori_loop(0, TOP_K * chunk, start, 0, unroll=8)
    lax.fori_loop(0, TOP_K * chunk, wait, 0, unroll=8)


def _dispatch(u3, pos_flat, n_slots, *, chunk=512):
    T = u3.shape[0]
    chunk = _tile(T, chunk)
    xs0 = jnp.zeros((n_slots,) + u3.shape[1:], u3.dtype)
    kern = functools.partial(_dispatch_kernel, chunk=chunk)
    return pl.pallas_call(
        kern,
        grid=(T // chunk,),
        in_specs=[
            pl.BlockSpec((TOP_K * chunk,), lambda i: (i,), memory_space=pltpu.SMEM),
            pl.BlockSpec((chunk,) + u3.shape[1:], lambda i: (i, 0, 0)),
            pl.BlockSpec(memory_space=pl.ANY),
        ],
        out_specs=pl.BlockSpec(memory_space=pl.ANY),
        out_shape=jax.ShapeDtypeStruct(xs0.shape, xs0.dtype),
        scratch_shapes=[pltpu.SemaphoreType.DMA(())],
        input_output_aliases={2: 0},
        compiler_params=_params(("arbitrary",), 4 * chunk * u3.shape[1] * u3.shape[2] * 2),
        name="moe_dispatch",
    )(pos_flat, u3, xs0)


def _combine_kernel(pos_ref, ys_ref, w0_ref, w1_ref, o_ref, buf_ref, sem, *, chunk):
    def copy(r):
        return pltpu.make_async_copy(ys_ref.at[pos_ref[r]], buf_ref.at[r % TOP_K, r // TOP_K], sem)

    def start(r, c):
        copy(r).start()
        return c

    def wait(r, c):
        copy(r).wait()
        return c

    lax.fori_loop(0, TOP_K * chunk, start, 0, unroll=8)
    lax.fori_loop(0, TOP_K * chunk, wait, 0, unroll=8)
    y = w0_ref[...] * buf_ref[0].astype(F32) + w1_ref[...] * buf_ref[1].astype(F32)
    o_ref[...] = y.astype(o_ref.dtype)


def _combine(ys3, pos_flat, w0, w1, n_tokens, *, chunk=512):
    tile = ys3.shape[1:]
    chunk = _tile(n_tokens, chunk)
    kern = functools.partial(_combine_kernel, chunk=chunk)
    wspec = pl.BlockSpec((chunk, 1, LANES), lambda i: (i, 0, 0))
    return pl.pallas_call(
        kern,
        grid=(n_tokens // chunk,),
        in_specs=[
            pl.BlockSpec((TOP_K * chunk,), lambda i: (i,), memory_space=pltpu.SMEM),
            pl.BlockSpec(memory_space=pl.ANY),
            wspec, wspec,
        ],
        out_specs=pl.BlockSpec((chunk,) + tile, lambda i: (i, 0, 0)),
        out_shape=jax.ShapeDtypeStruct((n_tokens,) + tile, BF16),
        scratch_shapes=[pltpu.VMEM((TOP_K, chunk) + tile, BF16), pltpu.SemaphoreType.DMA(())],
        compiler_params=_params(("arbitrary",), 8 * chunk * tile[0] * tile[1] * 4),
        name="moe_combine",
    )(pos_flat, ys3, w0.reshape(n_tokens, 1, LANES), w1.reshape(n_tokens, 1, LANES))


def _residual_kernel(h_ref, y_ref, gate_ref, o_ref):
    o_ref[...] = h_ref[...] + gate_ref[...] * y_ref[...].astype(F32)


def _gated_residual(h, y, gate, seq, *, tm=512):
    T, D = h.shape
    tm = _tile(seq, tm)
    rows_per_b = seq // tm
    return pl.pallas_call(
        _residual_kernel,
        grid=(T // tm,),
        in_specs=[
            pl.BlockSpec((tm, D), lambda i: (i, 0)),
            pl.BlockSpec((tm, D), lambda i: (i, 0)),
            pl.BlockSpec((None, 1, D), lambda i: (i // rows_per_b, 0, 0)),
        ],
        out_specs=pl.BlockSpec((tm, D), lambda i: (i, 0)),
        out_shape=jax.ShapeDtypeStruct((T, D), F32),
        compiler_params=_params(("arbitrary",), 6 * tm * D * 4),
        name="gated_residual",
    )(h, y, gate)


def _moe_layer(h, gain, scale, shift, gate, w_router, w13, w2, seq, *, tm=512):
    T, D = h.shape
    E = w_router.shape[1]
    u, route, w0, w1, counts = _router(h, gain, scale, shift, w_router, seq)

    expert = route[:, :TOP_K].astype(jnp.int32)
    rank = route[:, TOP_K:2 * TOP_K].astype(jnp.int32)
    count = counts[0, :E].astype(jnp.int32)
    padded = ((count + tm - 1) // tm) * tm
    ends = jnp.cumsum(padded)
    starts = ends - padded
    pos_flat = (starts[expert] + rank).reshape(-1)
    n_tiles = (TOP_K * T) // tm + E
    tile_start = jnp.arange(n_tiles, dtype=jnp.int32) * tm
    n_used = (ends[-1] // tm).astype(jnp.int32).reshape(1)
    tile_expert = jnp.minimum(jnp.searchsorted(ends, tile_start, side="right"), E - 1).astype(jnp.int32)
    tile_expert = jnp.where(tile_start < ends[-1], tile_expert, tile_expert[jnp.maximum(n_used[0] - 1, 0)])

    row_tile = (D // LANES, LANES)
    assert row_tile[0] % SUBLANES_BF16 == 0
    xs3 = _dispatch(u.reshape((T,) + row_tile), pos_flat, n_tiles * tm)
    ys = _expert_ffn(xs3.reshape(n_tiles * tm, D), tile_expert, n_used, w13, w2, tm=tm)
    y3 = _combine(ys.reshape((n_tiles * tm,) + row_tile), pos_flat, w0, w1, T)
    return _gated_residual(h, y3.reshape(T, D), gate, seq)


def kernel(x, c, norm_gain, mod_w, mod_b, ret_w_in, ret_w_out, fox_w_q, fox_q_gain, fox_w_out,
           kv_norm_gain, kv_mod_w, kv_mod_b, kv_w, kv_forget_bias, kv_k_gain,
           ffn_w13, ffn_w2, moe_router, moe_w13, moe_w2):
    B, S, D = x.shape
    T = B * S
    depth = mod_w.shape[0]
    n_ret = ret_w_in.shape[0]
    ret_heads = ret_w_out.shape[1] // RET_DV
    fox_heads = fox_w_q.shape[2] // FOX_DH
    fox_width = fox_heads * FOX_DH

    c8 = jnp.zeros((8, D), F32).at[:B].set(c)
    mod = _mod_call(c8, mod_w, mod_b)[:, :B]
    kv_mod = _mod_call(c8, kv_mod_w[None], kv_mod_b[None])[0, :B]

    def vecs(m, n):
        return [v.reshape(B, 1, D) for v in jnp.split(m, n, axis=-1)]

    h = x.reshape(T, D)
    kv_sh = decay_sh = None
    for i in range(depth):
        sh1, sc1, g1, sh2, sc2, g2 = vecs(mod[i], 6)
        if i < n_ret:
            proj = _nm_matmul(h, norm_gain[i, 0], sc1, sh1, ret_w_in[i].astype(BF16), S)
            y = _retention(proj, B, S, ret_heads)
            h = _proj_residual(y, ret_w_out[i].astype(BF16), h, g1, S)
        else:
            j = i - n_ret
            q = _nm_matmul(h, norm_gain[i, 0], sc1, sh1, fox_w_q[j].astype(BF16), S,
                           head_gain=fox_q_gain[j], norm_cols=fox_width, head_scale=LOG2E * FOX_DH ** -0.5)
            y = _fox_attention(q, kv_sh, decay_sh, B, S, fox_heads)
            h = _proj_residual(y, fox_w_out[j].astype(BF16), h, g1, S, tn=1024)
        if i % 2 == 0:
            h = _dense_ffn(h, norm_gain[i, 1], sc2, sh2, g2,
                           ffn_w13[i // 2].astype(BF16), ffn_w2[i // 2].astype(BF16), S)
        else:
            h = _moe_layer(h, norm_gain[i, 1], sc2, sh2, g2, moe_router[i // 2],
                           moe_w13[i // 2].astype(BF16), moe_w2[i // 2].astype(BF16), S)
        if i == n_ret - 1:
            ksh, ksc = vecs(kv_mod, 2)
            kv_sh = _nm_matmul(h, kv_norm_gain, ksc, ksh, kv_w[:, :2 * fox_width].astype(BF16), S,
                               head_gain=kv_k_gain, norm_cols=fox_width)
            decay_sh = _forget_cumsum(h, kv_norm_gain, ksc, ksh, kv_w[:, 2 * fox_width:], kv_forget_bias, S)
    return h.reshape(B, S, D)
```

```python
import functools
import math

import jax
import jax.numpy as jnp
from jax import lax
from jax.experimental import pallas as pl
from jax.experimental.pallas import tpu as pltpu

F32 = jnp.float32
BF16 = jnp.bfloat16
HIGHEST = lax.Precision.HIGHEST

EPS = 1e-6
ROPE_BASE = 10000.0
RET_DK = 256
RET_DV = 2 * RET_DK
FOX_DH = 128
TOP_K = 2
LOG2E = math.log2(math.e)
LANES = 128
SUBLANES_BF16 = 16
VMEM_LIMIT_CAP = 56 * 1024 * 1024


def _params(semantics, vmem_bytes):
    limit = int(min(max(2 * vmem_bytes, 16 * 1024 * 1024), VMEM_LIMIT_CAP))
    return pltpu.CompilerParams(dimension_semantics=semantics, vmem_limit_bytes=limit)


def _tile(n, pref):
    t = min(n, pref)
    assert n % t == 0, (n, pref)
    return t


def _row_tile(d):
    assert d % (SUBLANES_BF16 * LANES) == 0, d
    return (d // LANES, LANES)


def _norm_mod(h, gain, scale, shift):
    ms = jnp.mean(h * h, axis=-1, keepdims=True)
    return h * lax.rsqrt(ms + EPS) * gain * (1.0 + scale) + shift


def _mod_kernel(c_ref, w_ref, b_ref, o_ref):
    c = c_ref[...]
    cond = c * jax.nn.sigmoid(c)
    o_ref[...] = jnp.dot(cond, w_ref[...], precision=HIGHEST, preferred_element_type=F32) + b_ref[...]


def _mod_call(c8, w, b):
    L, D, N = w.shape
    tn = _tile(N, 1024)
    return pl.pallas_call(
        _mod_kernel,
        grid=(L, N // tn),
        in_specs=[
            pl.BlockSpec((8, D), lambda l, j: (0, 0)),
            pl.BlockSpec((None, D, tn), lambda l, j: (l, 0, j)),
            pl.BlockSpec((None, 1, tn), lambda l, j: (l, 0, j)),
        ],
        out_specs=pl.BlockSpec((None, 8, tn), lambda l, j: (l, 0, j)),
        out_shape=jax.ShapeDtypeStruct((L, 8, N), F32),
        compiler_params=_params(("arbitrary", "arbitrary"), 2 * D * tn * 4),
        name="adaln_mod",
    )(c8, w, b.reshape(L, 1, N))


def _nm_matmul_kernel(h_ref, gain_ref, scale_ref, shift_ref, w_ref, hg_ref, o_ref, u_ref,
                      *, n_norm_tiles, head_scale):
    j = pl.program_id(1)

    @pl.when(j == 0)
    def _():
        u_ref[...] = _norm_mod(h_ref[...], gain_ref[...], scale_ref[...], shift_ref[...]).astype(BF16)

    acc = jnp.dot(u_ref[...], w_ref[...], preferred_element_type=F32)

    if n_norm_tiles == 0:
        o_ref[...] = acc.astype(o_ref.dtype)
    else:
        @pl.when(j < n_norm_tiles)
        def _():
            hg = hg_ref[...] * head_scale
            for c in range(acc.shape[1] // FOX_DH):
                blk = acc[:, c * FOX_DH:(c + 1) * FOX_DH]
                ms = jnp.mean(blk * blk, axis=-1, keepdims=True)
                o_ref[:, c * FOX_DH:(c + 1) * FOX_DH] = (blk * lax.rsqrt(ms + EPS) * hg).astype(o_ref.dtype)

        @pl.when(j >= n_norm_tiles)
        def _():
            o_ref[...] = acc.astype(o_ref.dtype)


def _nm_matmul(h, gain, scale, shift, w, seq, *, head_gain=None, norm_cols=0, head_scale=1.0,
               tm=1024, tn=1024):
    T, D = h.shape
    N = w.shape[1]
    tm, tn = _tile(seq, tm), _tile(N, tn)
    assert norm_cols % tn == 0
    if head_gain is None:
        head_gain = jnp.ones((FOX_DH,), F32)
    rows_per_b = seq // tm
    vmem = 2 * tm * D * 4 + 2 * D * tn * 2 + 2 * tm * tn * 2 + tm * D * 2
    kern = functools.partial(_nm_matmul_kernel, n_norm_tiles=norm_cols // tn, head_scale=head_scale)
    return pl.pallas_call(
        kern,
        grid=(T // tm, N // tn),
        in_specs=[
            pl.BlockSpec((tm, D), lambda i, j: (i, 0)),
            pl.BlockSpec((1, D), lambda i, j: (0, 0)),
            pl.BlockSpec((None, 1, D), lambda i, j: (i // rows_per_b, 0, 0)),
            pl.BlockSpec((None, 1, D), lambda i, j: (i // rows_per_b, 0, 0)),
            pl.BlockSpec((D, tn), lambda i, j: (0, j)),
            pl.BlockSpec((1, FOX_DH), lambda i, j: (0, 0)),
        ],
        out_specs=pl.BlockSpec((tm, tn), lambda i, j: (i, j)),
        out_shape=jax.ShapeDtypeStruct((T, N), BF16),
        scratch_shapes=[pltpu.VMEM((tm, D), BF16)],
        compiler_params=_params(("arbitrary", "arbitrary"), vmem),
        name="norm_mod_proj",
    )(h, gain.reshape(1, D), scale, shift, w, head_gain.reshape(1, FOX_DH))


def _proj_residual_kernel(y_ref, w_ref, h_ref, gate_ref, o_ref):
    acc = jnp.dot(y_ref[...], w_ref[...], preferred_element_type=F32)
    o_ref[...] = h_ref[...] + gate_ref[...] * acc


def _proj_residual(y, w, h, gate, seq, *, tm=1024, tn=512):
    T, K = y.shape
    D = w.shape[1]
    tm, tn = _tile(seq, tm), _tile(D, tn)
    rows_per_b = seq // tm
    vmem = 2 * tm * K * 2 + 2 * K * tn * 2 + 4 * tm * tn * 4
    return pl.pallas_call(
        _proj_residual_kernel,
        grid=(T // tm, D // tn),
        in_specs=[
            pl.BlockSpec((tm, K), lambda i, j: (i, 0)),
            pl.BlockSpec((K, tn), lambda i, j: (0, j)),
            pl.BlockSpec((tm, tn), lambda i, j: (i, j)),
            pl.BlockSpec((None, 1, tn), lambda i, j: (i // rows_per_b, 0, j)),
        ],
        out_specs=pl.BlockSpec((tm, tn), lambda i, j: (i, j)),
        out_shape=jax.ShapeDtypeStruct((T, D), F32),
        compiler_params=_params(("arbitrary", "arbitrary"), vmem),
        name="proj_residual",
    )(y, w, h, gate)


def _rotary(x, cos, sin):
    half = x.shape[-1] // 2
    x1, x2 = x[:, :half], x[:, half:]
    return jnp.concatenate([x1 * cos - x2 * sin, x1 * sin + x2 * cos], axis=-1)


def _retention_kernel(lg_ref, q_ref, k_ref, v_ref, g_ref, cos_ref, sin_ref, o_ref, state_ref,
                      *, chunk, n_chunks):
    @pl.when(pl.program_id(2) == 0)
    def _():
        state_ref[...] = jnp.zeros_like(state_ref)

    lg = lg_ref[pl.program_id(1)]
    row = lax.broadcasted_iota(jnp.int32, (chunk, chunk), 0)
    col = lax.broadcasted_iota(jnp.int32, (chunk, chunk), 1)
    dist = (row - col).astype(F32)
    d_local = jnp.where(dist >= 0, jnp.exp(lg * jnp.maximum(dist, 0.0)), 0.0)
    t = lax.broadcasted_iota(jnp.int32, (chunk, 1), 0).astype(F32)
    xi = jnp.exp(lg * (t + 1.0))
    zeta = jnp.exp(lg * (chunk - 1.0 - t))
    gamma_c = jnp.exp(jnp.full((1, 1), lg * chunk, F32))

    def body(ci, carry):
        r0 = pl.multiple_of(ci * chunk, chunk)
        cos = cos_ref[pl.ds(r0, chunk), :]
        sin = sin_ref[pl.ds(r0, chunk), :]
        q = _rotary(q_ref[pl.ds(r0, chunk), :].astype(F32), cos, sin).astype(BF16)
        k = (_rotary(k_ref[pl.ds(r0, chunk), :].astype(F32), cos, sin) * (RET_DK ** -0.5)).astype(BF16)
        v = v_ref[pl.ds(r0, chunk), :]
        scores = lax.dot_general(q, k, (((1,), (1,)), ((), ())), preferred_element_type=F32) * d_local
        inner = jnp.dot(scores.astype(BF16), v, preferred_element_type=F32)
        state = state_ref[...]
        cross = jnp.dot(q, state.astype(BF16), preferred_element_type=F32) * xi
        vz = (v.astype(F32) * zeta).astype(BF16)
        state_ref[...] = gamma_c * state + lax.dot_general(
            k, vz, (((0,), (0,)), ((), ())), preferred_element_type=F32)
        o = inner + cross
        ms = jnp.mean(o * o, axis=-1, keepdims=True)
        g = g_ref[pl.ds(r0, chunk), :].astype(F32)
        o_ref[pl.ds(r0, chunk), :] = (g * jax.nn.sigmoid(g) * (o * lax.rsqrt(ms + EPS))).astype(o_ref.dtype)
        return carry

    lax.fori_loop(0, n_chunks, body, 0)


def _retention(proj, batch, seq, heads, *, chunk=256, rows=1024):
    T = proj.shape[0]
    chunk = _tile(seq, chunk)
    rows = _tile(seq, rows)
    nblk = seq // rows
    half = RET_DK // 2
    inv = 1.0 / (ROPE_BASE ** (jnp.arange(half, dtype=F32) / half))
    ang = jnp.arange(seq, dtype=F32)[:, None] * inv[None, :]
    cos, sin = jnp.cos(ang), jnp.sin(ang)
    log_gamma = jnp.log1p(-jnp.exp2(-5.0 - jnp.arange(heads, dtype=F32)))
    kern = functools.partial(_retention_kernel, chunk=chunk, n_chunks=rows // chunk)
    qk_blocks = heads
    v_off = 2 * heads * RET_DK // RET_DV
    return pl.pallas_call(
        kern,
        grid=(batch, heads, nblk),
        in_specs=[
            pl.BlockSpec(memory_space=pltpu.SMEM),
            pl.BlockSpec((rows, RET_DK), lambda b, h, c: (b * nblk + c, h)),
            pl.BlockSpec((rows, RET_DK), lambda b, h, c: (b * nblk + c, qk_blocks + h)),
            pl.BlockSpec((rows, RET_DV), lambda b, h, c: (b * nblk + c, v_off + h)),
            pl.BlockSpec((rows, RET_DV), lambda b, h, c: (b * nblk + c, v_off + heads + h)),
            pl.BlockSpec((rows, half), lambda b, h, c: (c, 0)),
            pl.BlockSpec((rows, half), lambda b, h, c: (c, 0)),
        ],
        out_specs=pl.BlockSpec((rows, RET_DV), lambda b, h, c: (b * nblk + c, h)),
        out_shape=jax.ShapeDtypeStruct((T, heads * RET_DV), BF16),
        scratch_shapes=[pltpu.VMEM((RET_DK, RET_DV), F32)],
        compiler_params=_params(("arbitrary", "arbitrary", "arbitrary"), 16 * 1024 * 1024),
        name="retention",
    )(log_gamma, proj, proj, proj, proj, cos, sin)


def _small_proj(h_ref, gain_ref, scale_ref, shift_ref, w_ref):
    u = _norm_mod(h_ref[...], gain_ref[...], scale_ref[...], shift_ref[...])
    return u, jnp.dot(u, w_ref[...], precision=HIGHEST, preferred_element_type=F32)


def _forget_kernel(h_ref, gain_ref, scale_ref, shift_ref, w_ref, bias_ref, o_ref, carry_ref,
                   *, tiles_per_seq, n_heads):
    @pl.when(pl.program_id(0) % tiles_per_seq == 0)
    def _():
        carry_ref[...] = jnp.zeros_like(carry_ref)

    _, f = _small_proj(h_ref, gain_ref, scale_ref, shift_ref, w_ref)
    log_f = jax.nn.log_sigmoid(f + bias_ref[...])
    tm = log_f.shape[0]
    row = lax.broadcasted_iota(jnp.int32, (tm, tm), 0)
    col = lax.broadcasted_iota(jnp.int32, (tm, tm), 1)
    tri = jnp.where(col <= row, 1.0, 0.0).astype(F32)
    cum = jnp.dot(tri, log_f, precision=HIGHEST, preferred_element_type=F32) + carry_ref[...]
    carry_ref[...] = cum[tm - 1:tm, :]
    neg = cum * (-LOG2E)
    lane = lax.broadcasted_iota(jnp.int32, (tm, FOX_DH), 1)
    for hh in range(n_heads):
        b = neg[:, hh:hh + 1]
        hi = b.astype(BF16).astype(F32)
        mid = (b - hi).astype(BF16).astype(F32)
        lo = b - hi - mid
        blk = jnp.where(lane == 0, hi, jnp.where(lane == 1, mid, jnp.where(lane == 2, lo, 0.0)))
        o_ref[:, hh * FOX_DH:(hh + 1) * FOX_DH] = blk.astype(o_ref.dtype)


def _forget_cumsum(h, gain, scale, shift, w_f, bias, seq, *, tm=512):
    T, D = h.shape
    nh = w_f.shape[1]
    tm = _tile(seq, tm)
    rows_per_b = seq // tm
    w_pad = jnp.zeros((D, LANES), F32).at[:, :nh].set(w_f)
    b_pad = jnp.zeros((1, LANES), F32).at[0, :nh].set(bias)
    kern = functools.partial(_forget_kernel, tiles_per_seq=rows_per_b, n_heads=nh)
    return pl.pallas_call(
        kern,
        grid=(T // tm,),
        in_specs=[
            pl.BlockSpec((tm, D), lambda i: (i, 0)),
            pl.BlockSpec((1, D), lambda i: (0, 0)),
            pl.BlockSpec((None, 1, D), lambda i: (i // rows_per_b, 0, 0)),
            pl.BlockSpec((None, 1, D), lambda i: (i // rows_per_b, 0, 0)),
            pl.BlockSpec((D, LANES), lambda i: (0, 0)),
            pl.BlockSpec((1, LANES), lambda i: (0, 0)),
        ],
        out_specs=pl.BlockSpec((tm, nh * FOX_DH), lambda i: (i, 0)),
        out_shape=jax.ShapeDtypeStruct((T, nh * FOX_DH), BF16),
        scratch_shapes=[pltpu.VMEM((1, LANES), F32)],
        compiler_params=_params(("arbitrary",), 2 * tm * D * 4 + 4 * tm * D * 4),
        name="forget_cumsum",
    )(h, gain.reshape(1, D), scale, shift, w_pad, b_pad)


def _router_kernel(h_ref, gain_ref, scale_ref, shift_ref, w_ref, u_ref, route_ref, w0_ref, w1_ref,
                   count_ref, carry_ref, *, n_experts):
    @pl.when(pl.program_id(0) == 0)
    def _():
        carry_ref[...] = jnp.zeros_like(carry_ref)

    u, logits = _small_proj(h_ref, gain_ref, scale_ref, shift_ref, w_ref)
    u_ref[...] = u.reshape(u_ref.shape).astype(u_ref.dtype)
    tm = logits.shape[0]
    lane = lax.broadcasted_iota(jnp.int32, (tm, LANES), 1)
    neg = -jnp.inf
    l0 = jnp.where(lane < n_experts, logits, neg)
    m0 = jnp.max(l0, axis=-1, keepdims=True)
    i0 = jnp.min(jnp.where(l0 == m0, lane, LANES), axis=-1, keepdims=True)
    l1 = jnp.where(lane == i0, neg, l0)
    m1 = jnp.max(l1, axis=-1, keepdims=True)
    i1 = jnp.min(jnp.where(l1 == m1, lane, LANES), axis=-1, keepdims=True)
    e = jnp.exp(m1 - m0)
    wgt0 = 1.0 / (1.0 + e)
    wgt1 = e / (1.0 + e)
    onehot = jnp.where(lane == i0, 1.0, jnp.where(lane == i1, 1.0, 0.0))
    row = lax.broadcasted_iota(jnp.int32, (tm, tm), 0)
    col = lax.broadcasted_iota(jnp.int32, (tm, tm), 1)
    tri = jnp.where(col < row, 1.0, 0.0).astype(BF16)
    before = jnp.dot(tri, onehot.astype(BF16), preferred_element_type=F32) + carry_ref[...]
    rank0 = jnp.sum(jnp.where(lane == i0, before, 0.0), axis=-1, keepdims=True)
    rank1 = jnp.sum(jnp.where(lane == i1, before, 0.0), axis=-1, keepdims=True)
    total = carry_ref[...] + jnp.sum(onehot, axis=0, keepdims=True)
    carry_ref[...] = total
    count_ref[...] = jnp.broadcast_to(total, count_ref.shape)
    route = jnp.where(lane == 0, i0.astype(F32),
                      jnp.where(lane == 1, i1.astype(F32),
                                jnp.where(lane == 2, rank0, jnp.where(lane == 3, rank1, 0.0))))
    route_ref[...] = route
    w0_ref[...] = jnp.broadcast_to(wgt0, w0_ref.shape)
    w1_ref[...] = jnp.broadcast_to(wgt1, w1_ref.shape)


def _router(h, gain, scale, shift, w_router, seq, *, tm=512):
    T, D = h.shape
    E = w_router.shape[1]
    tm = _tile(seq, tm)
    rows_per_b = seq // tm
    w_pad = jnp.zeros((D, LANES), F32).at[:, :E].set(w_router)
    kern = functools.partial(_router_kernel, n_experts=E)
    row_spec = pl.BlockSpec((tm, LANES), lambda i: (i, 0))
    row_shape = jax.ShapeDtypeStruct((T, LANES), F32)
    return pl.pallas_call(
        kern,
        grid=(T // tm,),
        in_specs=[
            pl.BlockSpec((tm, D), lambda i: (i, 0)),
            pl.BlockSpec((1, D), lambda i: (0, 0)),
            pl.BlockSpec((None, 1, D), lambda i: (i // rows_per_b, 0, 0)),
            pl.BlockSpec((None, 1, D), lambda i: (i // rows_per_b, 0, 0)),
            pl.BlockSpec((D, LANES), lambda i: (0, 0)),
        ],
        out_specs=[
            pl.BlockSpec((tm,) + _row_tile(D), lambda i: (i, 0, 0)),
            row_spec, row_spec, row_spec,
            pl.BlockSpec((8, LANES), lambda i: (0, 0)),
        ],
        out_shape=[
            jax.ShapeDtypeStruct((T,) + _row_tile(D), BF16),
            row_shape, row_shape, row_shape,
            jax.ShapeDtypeStruct((8, LANES), F32),
        ],
        scratch_shapes=[pltpu.VMEM((1, LANES), F32)],
        compiler_params=_params(("arbitrary",), 2 * tm * D * 4 + 2 * tm * D * 2 + 4 * tm * D * 4),
        name="moe_router",
    )(h, gain.reshape(1, D), scale, shift, w_pad)


def _fox_kernel(ti_ref, tj_ref, q_ref, k_ref, e_ref, v_ref, o_ref, kp_ref, vp_ref, s0_ref, s1_ref,
                p0_ref, p1_ref, a0_ref, a1_ref, m_ref, acc_ref, *, tq, n_off, n_diag):
    lane = lax.broadcasted_iota(jnp.int32, k_ref.shape, 1)
    kp_ref[:FOX_DH, :] = k_ref[...].T
    kp_ref[FOX_DH:, :] = e_ref[...].T
    vp_ref[:, :FOX_DH] = v_ref[...]
    vp_ref[:, FOX_DH:] = jnp.where(lane == 0, 1.0, 0.0).astype(BF16)
    m_ref[...] = jnp.full(m_ref.shape, -jnp.inf, F32)
    acc_ref[...] = jnp.zeros_like(acc_ref)

    STEPS_PER_GROUP = 2
    lane = lax.broadcasted_iota(jnp.int32, (tq, FOX_DH), 1)
    ones = jnp.where(lane < 3, 1.0, 0.0).astype(BF16)
    s_refs, p_refs, a_refs = (s0_ref, s1_ref), (p0_ref, p1_ref), (a0_ref, a1_ref)

    def tile_rows(idx_ref, t):
        return pl.ds(pl.multiple_of(idx_ref[t] * tq, tq), tq)

    def logits(t, par):
        qp = jnp.concatenate([q_ref[tile_rows(ti_ref, t), :], ones], axis=-1)
        kp = kp_ref[:, tile_rows(tj_ref, t)]
        s_refs[par][...] = jnp.dot(qp, kp, preferred_element_type=F32)

    def softmax(t, par, diag):
        rows = tile_rows(ti_ref, t)
        s = s_refs[par][...]
        if diag:
            row = lax.broadcasted_iota(jnp.int32, (tq, tq), 0)
            col = lax.broadcasted_iota(jnp.int32, (tq, tq), 1)
            s = jnp.where(col <= row, s, -jnp.inf)
        m = m_ref[rows, :]
        m_new = jnp.maximum(m, jnp.max(s, axis=-1, keepdims=True))
        m_ref[rows, :] = m_new
        a_refs[par][...] = jnp.exp2(m - m_new)
        p_refs[par][...] = jnp.exp2(s - m_new).astype(BF16)

    def values(t, par, diag):
        rows = tile_rows(ti_ref, t)
        vp = vp_ref[tile_rows(tj_ref, t), :]
        acc = a_refs[par][...] * acc_ref[rows, :] + jnp.dot(p_refs[par][...], vp,
                                                            preferred_element_type=F32)
        if diag:
            o_ref[rows, :] = (acc[:, :FOX_DH] / acc[:, FOX_DH:FOX_DH + 1]).astype(o_ref.dtype)
        else:
            acc_ref[rows, :] = acc

    def pipeline(t0, n, diag):
        def step(u, par):
            static = isinstance(u, int)
            if not static or 2 <= u <= n + 1:
                values(t0 + u - 2, par, diag)
            if not static or 1 <= u <= n:
                softmax(t0 + u - 1, 1 - par, diag)
            if not static or u <= n - 1:
                logits(t0 + u, par)

        def group(d, c):
            for g in range(STEPS_PER_GROUP):
                step(2 + STEPS_PER_GROUP * d + g, g % 2)
            return c

        head = min(2, n + 2)
        n_groups = max(n - 2, 0) // STEPS_PER_GROUP
        for u in range(head):
            step(u, u % 2)
        lax.fori_loop(0, n_groups, group, 0)
        for u in range(head + STEPS_PER_GROUP * n_groups, n + 2):
            step(u, u % 2)

    pipeline(0, n_off, False)
    pipeline(n_off, n_diag, True)


def _fox_attention(q, kv, e, batch, seq, heads, *, tq=512):
    T = q.shape[0]
    tq = _tile(seq, tq)
    nq = seq // tq
    below = [(i, j) for i in range(nq) for j in range(i)]
    tiles = below + [(i, i) for i in range(nq)]
    ti = jnp.array([t[0] for t in tiles], jnp.int32)
    tj = jnp.array([t[1] for t in tiles], jnp.int32)
    kern = functools.partial(_fox_kernel, tq=tq, n_off=len(below), n_diag=nq)
    grid_spec = pltpu.PrefetchScalarGridSpec(
        num_scalar_prefetch=2,
        grid=(batch, heads),
        in_specs=[
            pl.BlockSpec((seq, FOX_DH), lambda b, h, ti, tj: (b, h)),
            pl.BlockSpec((seq, FOX_DH), lambda b, h, ti, tj: (b, h)),
            pl.BlockSpec((seq, FOX_DH), lambda b, h, ti, tj: (b, h)),
            pl.BlockSpec((seq, FOX_DH), lambda b, h, ti, tj: (b, heads + h)),
        ],
        out_specs=pl.BlockSpec((seq, FOX_DH), lambda b, h, ti, tj: (b, h)),
        scratch_shapes=[pltpu.VMEM((2 * FOX_DH, seq), BF16), pltpu.VMEM((seq, 2 * FOX_DH), BF16),
                        pltpu.VMEM((tq, tq), F32), pltpu.VMEM((tq, tq), F32),
                        pltpu.VMEM((tq, tq), BF16), pltpu.VMEM((tq, tq), BF16),
                        pltpu.VMEM((tq, 1), F32), pltpu.VMEM((tq, 1), F32), pltpu.VMEM((seq, 1), F32),
                        pltpu.VMEM((seq, 2 * FOX_DH), F32)],
    )
    vmem = 14 * seq * FOX_DH * 2 + seq * (LANES + 2 * FOX_DH) * 4 + 8 * tq * tq * 4
    return pl.pallas_call(
        kern,
        grid_spec=grid_spec,
        out_shape=jax.ShapeDtypeStruct((T, heads * FOX_DH), BF16),
        compiler_params=_params(("arbitrary", "arbitrary"), vmem),
        name="fox_attention",
    )(ti, tj, q, kv, e, kv)


def _swiglu_step(u, w1_ref, w3_ref, w2_ref, acc_ref):
    a = jnp.dot(u, w1_ref[...].astype(BF16), preferred_element_type=F32)
    g = jnp.dot(u, w3_ref[...].astype(BF16), preferred_element_type=F32)
    mid = (a * jax.nn.sigmoid(a) * g).astype(BF16)
    acc_ref[...] += jnp.dot(mid, w2_ref[...].astype(BF16), preferred_element_type=F32)


def _dense_ffn_kernel(h_ref, gain_ref, scale_ref, shift_ref, gate_ref, w1_ref, w3_ref, w2_ref,
                      o_ref, u_ref, acc_ref):
    f = pl.program_id(1)

    @pl.when(f == 0)
    def _():
        u_ref[...] = _norm_mod(h_ref[...], gain_ref[...], scale_ref[...], shift_ref[...]).astype(BF16)
        acc_ref[...] = jnp.zeros_like(acc_ref)

    _swiglu_step(u_ref[...], w1_ref, w3_ref, w2_ref, acc_ref)

    @pl.when(f == pl.num_programs(1) - 1)
    def _():
        o_ref[...] = h_ref[...] + gate_ref[...] * acc_ref[...]


def _dense_ffn(h, gain, scale, shift, gate, w13, w2, seq, *, tm=512, tf=512):
    T, D = h.shape
    F = w2.shape[0]
    tm, tf = _tile(seq, tm), _tile(F, tf)
    nf = F // tf
    rows_per_b = seq // tm
    vec = pl.BlockSpec((None, 1, D), lambda i, f: (i // rows_per_b, 0, 0))
    vmem = 4 * tm * D * 4 + tm * D * 2 + tm * D * 4 + 2 * 3 * D * tf * 2
    return pl.pallas_call(
        _dense_ffn_kernel,
        grid=(T // tm, nf),
        in_specs=[
            pl.BlockSpec((tm, D), lambda i, f: (i, 0)),
            pl.BlockSpec((1, D), lambda i, f: (0, 0)),
            vec, vec, vec,
            pl.BlockSpec((D, tf), lambda i, f: (0, f)),
            pl.BlockSpec((D, tf), lambda i, f: (0, nf + f)),
            pl.BlockSpec((tf, D), lambda i, f: (f, 0)),
        ],
        out_specs=pl.BlockSpec((tm, D), lambda i, f: (i, 0)),
        out_shape=jax.ShapeDtypeStruct((T, D), F32),
        scratch_shapes=[pltpu.VMEM((tm, D), BF16), pltpu.VMEM((tm, D), F32)],
        compiler_params=_params(("arbitrary", "arbitrary"), vmem),
        name="dense_swiglu",
    )(h, gain.reshape(1, D), scale, shift, gate, w13, w13, w2)


def _expert_ffn_kernel(te_ref, nu_ref, x_ref, w1_ref, w3_ref, w2_ref, o_ref, x2_ref, acc_ref):
    i, f = pl.program_id(0), pl.program_id(1)
    used = i < nu_ref[0]

    @pl.when(f == 0)
    def _():
        x2_ref[...] = x_ref[...].astype(F32).reshape(x2_ref.shape).astype(x2_ref.dtype)
        acc_ref[...] = jnp.zeros_like(acc_ref)

    @pl.when(used)
    def _():
        _swiglu_step(x2_ref[...], w1_ref, w3_ref, w2_ref, acc_ref)

    @pl.when(f == pl.num_programs(1) - 1)
    def _():
        o_ref[...] = acc_ref[...].reshape(o_ref.shape).astype(o_ref.dtype)


def _expert_ffn(xs, tile_expert, n_used, w13, w2, layer, *, tm, tf=256):
    P = xs.shape[0]
    row_tile = xs.shape[1:]
    _, E, F, D = w2.shape
    tf = _tile(F, tf)
    nf = F // tf

    def fidx(i, f, nu):
        return jnp.where(i < nu[0], f, nf - 1)

    grid_spec = pltpu.PrefetchScalarGridSpec(
        num_scalar_prefetch=2,
        grid=(P // tm, nf),
        in_specs=[
            pl.BlockSpec((tm,) + row_tile, lambda i, f, te, nu: (i, 0, 0)),
            pl.BlockSpec((None, None, D, tf), lambda i, f, te, nu: (layer, te[i], 0, fidx(i, f, nu))),
            pl.BlockSpec((None, None, D, tf), lambda i, f, te, nu: (layer, te[i], 0, nf + fidx(i, f, nu))),
            pl.BlockSpec((None, None, tf, D), lambda i, f, te, nu: (layer, te[i], fidx(i, f, nu), 0)),
        ],
        out_specs=pl.BlockSpec((tm,) + row_tile, lambda i, f, te, nu: (i, 0, 0)),
        scratch_shapes=[pltpu.VMEM((tm, D), BF16), pltpu.VMEM((tm, D), F32)],
    )
    w_bytes = jnp.dtype(w2.dtype).itemsize
    vmem = 5 * tm * D * 2 + tm * D * 4 + 3 * D * tf * (2 * w_bytes + 2) + 4 * tm * tf * 4
    return pl.pallas_call(
        _expert_ffn_kernel,
        grid_spec=grid_spec,
        out_shape=jax.ShapeDtypeStruct((P,) + row_tile, BF16),
        compiler_params=_params(("arbitrary", "arbitrary"), vmem),
        name="expert_swiglu",
    )(tile_expert, n_used, xs, w13, w13, w2)


def _dispatch_kernel(pos_ref, u_ref, xs_in_ref, xs_ref, sem, *, chunk):
    del xs_in_ref

    def copy(r):
        return pltpu.make_async_copy(u_ref.at[r // TOP_K], xs_ref.at[pos_ref[r]], sem)

    def start(r, c):
        copy(r).start()
        return c

    def wait(r, c):
        copy(r).wait()
        return c

    lax.fori_loop(0, TOP_K * chunk, start, 0, unroll=8)
    lax.fori_loop(0, TOP_K * chunk, wait, 0, unroll=8)


def _dispatch(u3, pos_flat, n_slots, *, chunk=512):
    T = u3.shape[0]
    chunk = _tile(T, chunk)
    xs0 = jnp.zeros((n_slots,) + u3.shape[1:], u3.dtype)
    kern = functools.partial(_dispatch_kernel, chunk=chunk)
    return pl.pallas_call(
        kern,
        grid=(T // chunk,),
        in_specs=[
            pl.BlockSpec((TOP_K * chunk,), lambda i: (i,), memory_space=pltpu.SMEM),
            pl.BlockSpec((chunk,) + u3.shape[1:], lambda i: (i, 0, 0)),
            pl.BlockSpec(memory_space=pl.ANY),
        ],
        out_specs=pl.BlockSpec(memory_space=pl.ANY),
        out_shape=jax.ShapeDtypeStruct(xs0.shape, xs0.dtype),
        scratch_shapes=[pltpu.SemaphoreType.DMA(())],
        input_output_aliases={2: 0},
        compiler_params=_params(("arbitrary",), 4 * chunk * u3.shape[1] * u3.shape[2] * 2),
        name="moe_dispatch",
    )(pos_flat, u3, xs0)


def _combine_kernel(pos_ref, ys_ref, w0_ref, w1_ref, h_ref, gate_ref, o_ref, buf_ref, sem, *, chunk):
    def copy(r):
        return pltpu.make_async_copy(ys_ref.at[pos_ref[r]], buf_ref.at[r % TOP_K, r // TOP_K], sem)

    def start(r, c):
        copy(r).start()
        return c

    def wait(r, c):
        copy(r).wait()
        return c

    lax.fori_loop(0, TOP_K * chunk, start, 0, unroll=8)
    lax.fori_loop(0, TOP_K * chunk, wait, 0, unroll=8)
    y = w0_ref[...] * buf_ref[0].astype(F32) + w1_ref[...] * buf_ref[1].astype(F32)
    o_ref[...] = h_ref[...] + gate_ref[...] * y.reshape(o_ref.shape)


def _combine(ys3, pos_flat, w0, w1, h, gate, seq, *, chunk=512):
    T, D = h.shape
    tile = ys3.shape[1:]
    chunk = _tile(seq, chunk)
    rows_per_b = seq // chunk
    kern = functools.partial(_combine_kernel, chunk=chunk)
    wspec = pl.BlockSpec((chunk, 1, LANES), lambda i: (i, 0, 0))
    return pl.pallas_call(
        kern,
        grid=(T // chunk,),
        in_specs=[
            pl.BlockSpec((TOP_K * chunk,), lambda i: (i,), memory_space=pltpu.SMEM),
            pl.BlockSpec(memory_space=pl.ANY),
            wspec, wspec,
            pl.BlockSpec((chunk, D), lambda i: (i, 0)),
            pl.BlockSpec((None, 1, D), lambda i: (i // rows_per_b, 0, 0)),
        ],
        out_specs=pl.BlockSpec((chunk, D), lambda i: (i, 0)),
        out_shape=jax.ShapeDtypeStruct((T, D), F32),
        scratch_shapes=[pltpu.VMEM((TOP_K, chunk) + tile, BF16), pltpu.SemaphoreType.DMA(())],
        compiler_params=_params(("arbitrary",), 10 * chunk * D * 4),
        name="moe_combine",
    )(pos_flat, ys3, w0.reshape(T, 1, LANES), w1.reshape(T, 1, LANES), h, gate)


def _moe_layer(h, gain, scale, shift, gate, w_router, w13, w2, layer, seq, *, tm=1024):
    T, D = h.shape
    E = w_router.shape[1]
    u3, route, w0, w1, counts = _router(h, gain, scale, shift, w_router, seq)

    expert = route[:, :TOP_K].astype(jnp.int32)
    rank = route[:, TOP_K:2 * TOP_K].astype(jnp.int32)
    count = counts[0, :E].astype(jnp.int32)
    padded = ((count + tm - 1) // tm) * tm
    ends = jnp.cumsum(padded)
    starts = ends - padded
    pos_flat = (starts[expert] + rank).reshape(-1)
    n_tiles = (TOP_K * T) // tm + E
    tile_start = jnp.arange(n_tiles, dtype=jnp.int32) * tm
    n_used = (ends[-1] // tm).astype(jnp.int32).reshape(1)
    tile_expert = jnp.minimum(jnp.searchsorted(ends, tile_start, side="right"), E - 1).astype(jnp.int32)
    tile_expert = jnp.where(tile_start < ends[-1], tile_expert, tile_expert[jnp.maximum(n_used[0] - 1, 0)])

    xs3 = _dispatch(u3, pos_flat, n_tiles * tm)
    ys3 = _expert_ffn(xs3, tile_expert, n_used, w13, w2, layer, tm=tm)
    return _combine(ys3, pos_flat, w0, w1, h, gate, seq)


def kernel(x, c, norm_gain, mod_w, mod_b, ret_w_in, ret_w_out, fox_w_q, fox_q_gain, fox_w_out,
           kv_norm_gain, kv_mod_w, kv_mod_b, kv_w, kv_forget_bias, kv_k_gain,
           ffn_w13, ffn_w2, moe_router, moe_w13, moe_w2):
    B, S, D = x.shape
    T = B * S
    depth = mod_w.shape[0]
    n_ret = ret_w_in.shape[0]
    ret_heads = ret_w_out.shape[1] // RET_DV
    fox_heads = fox_w_q.shape[2] // FOX_DH
    fox_width = fox_heads * FOX_DH

    c8 = jnp.zeros((8, D), F32).at[:B].set(c)
    mod = _mod_call(c8, mod_w, mod_b)[:, :B]
    kv_mod = _mod_call(c8, kv_mod_w[None], kv_mod_b[None])[0, :B]

    def vecs(m, n):
        return [v.reshape(B, 1, D) for v in jnp.split(m, n, axis=-1)]

    h = x.reshape(T, D)
    kv_sh = decay_sh = None
    for i in range(depth):
        sh1, sc1, g1, sh2, sc2, g2 = vecs(mod[i], 6)
        if i < n_ret:
            proj = _nm_matmul(h, norm_gain[i, 0], sc1, sh1, ret_w_in[i].astype(BF16), S)
            y = _retention(proj, B, S, ret_heads)
            h = _proj_residual(y, ret_w_out[i].astype(BF16), h, g1, S)
        else:
            j = i - n_ret
            q = _nm_matmul(h, norm_gain[i, 0], sc1, sh1, fox_w_q[j].astype(BF16), S,
                           head_gain=fox_q_gain[j], norm_cols=fox_width, head_scale=LOG2E * FOX_DH ** -0.5)
            y = _fox_attention(q, kv_sh, decay_sh, B, S, fox_heads)
            h = _proj_residual(y, fox_w_out[j].astype(BF16), h, g1, S, tn=1024)
        if i % 2 == 0:
            h = _dense_ffn(h, norm_gain[i, 1], sc2, sh2, g2,
                           ffn_w13[i // 2].astype(BF16), ffn_w2[i // 2].astype(BF16), S)
        else:
            h = _moe_layer(h, norm_gain[i, 1], sc2, sh2, g2, moe_router[i // 2], moe_w13, moe_w2, i // 2, S)
        if i == n_ret - 1:
            ksh, ksc = vecs(kv_mod, 2)
            kv_sh = _nm_matmul(h, kv_norm_gain, ksc, ksh, kv_w[:, :2 * fox_width].astype(BF16), S,
                               head_gain=kv_k_gain, norm_cols=fox_width)
            decay_sh = _forget_cumsum(h, kv_norm_gain, ksc, ksh, kv_w[:, 2 * fox_width:], kv_forget_bias, S)
    return h.reshape(B, S, D)
```

```python
import functools
import math

import jax
import jax.numpy as jnp
from jax import lax
from jax.experimental import pallas as pl
from jax.experimental.pallas import tpu as pltpu

F32 = jnp.float32
BF16 = jnp.bfloat16
HIGHEST = lax.Precision.HIGHEST

EPS = 1e-6
ROPE_BASE = 10000.0
RET_DK = 256
RET_DV = 2 * RET_DK
FOX_DH = 128
FOX_PIPELINE_LAG = 2
TOP_K = 2
LOG2E = math.log2(math.e)
LANES = 128
SUBLANES_BF16 = 16
VMEM_LIMIT_CAP = 56 * 1024 * 1024


def _params(semantics, vmem_bytes):
    limit = int(min(max(2 * vmem_bytes, 16 * 1024 * 1024), VMEM_LIMIT_CAP))
    return pltpu.CompilerParams(dimension_semantics=semantics, vmem_limit_bytes=limit)


def _tile(n, pref):
    t = min(n, pref)
    assert n % t == 0, (n, pref)
    return t


def _row_tile(d):
    assert d % (SUBLANES_BF16 * LANES) == 0, d
    return (d // LANES, LANES)


def _norm_mod(h, gain, scale, shift):
    ms = jnp.mean(h * h, axis=-1, keepdims=True)
    return h * lax.rsqrt(ms + EPS) * gain * (1.0 + scale) + shift


def _mod_kernel(c_ref, w_ref, b_ref, o_ref):
    c = c_ref[...]
    cond = c * jax.nn.sigmoid(c)
    o_ref[...] = jnp.dot(cond, w_ref[...], precision=HIGHEST, preferred_element_type=F32) + b_ref[...]


def _mod_call(c8, w, b):
    L, D, N = w.shape
    tn = _tile(N, 1024)
    return pl.pallas_call(
        _mod_kernel,
        grid=(L, N // tn),
        in_specs=[
            pl.BlockSpec((8, D), lambda l, j: (0, 0)),
            pl.BlockSpec((None, D, tn), lambda l, j: (l, 0, j)),
            pl.BlockSpec((None, 1, tn), lambda l, j: (l, 0, j)),
        ],
        out_specs=pl.BlockSpec((None, 8, tn), lambda l, j: (l, 0, j)),
        out_shape=jax.ShapeDtypeStruct((L, 8, N), F32),
        compiler_params=_params(("arbitrary", "arbitrary"), 2 * D * tn * 4),
        name="adaln_mod",
    )(c8, w, b.reshape(L, 1, N))


def _nm_matmul_kernel(h_ref, gain_ref, scale_ref, shift_ref, w_ref, hg_ref, o_ref, u_ref,
                      *, n_norm_tiles, head_scale):
    j = pl.program_id(1)

    @pl.when(j == 0)
    def _():
        u_ref[...] = _norm_mod(h_ref[...], gain_ref[...], scale_ref[...], shift_ref[...]).astype(BF16)

    acc = jnp.dot(u_ref[...], w_ref[...], preferred_element_type=F32)

    if n_norm_tiles == 0:
        o_ref[...] = acc.astype(o_ref.dtype)
    else:
        @pl.when(j < n_norm_tiles)
        def _():
            hg = hg_ref[...] * head_scale
            for c in range(acc.shape[1] // FOX_DH):
                blk = acc[:, c * FOX_DH:(c + 1) * FOX_DH]
                ms = jnp.mean(blk * blk, axis=-1, keepdims=True)
                o_ref[:, c * FOX_DH:(c + 1) * FOX_DH] = (blk * lax.rsqrt(ms + EPS) * hg).astype(o_ref.dtype)

        @pl.when(j >= n_norm_tiles)
        def _():
            o_ref[...] = acc.astype(o_ref.dtype)


def _nm_matmul(h, gain, scale, shift, w, seq, *, head_gain=None, norm_cols=0, head_scale=1.0,
               tm=1024, tn=1024):
    T, D = h.shape
    N = w.shape[1]
    tm, tn = _tile(seq, tm), _tile(N, tn)
    assert norm_cols % tn == 0
    if head_gain is None:
        head_gain = jnp.ones((FOX_DH,), F32)
    rows_per_b = seq // tm
    vmem = 2 * tm * D * 4 + 2 * D * tn * 2 + 2 * tm * tn * 2 + tm * D * 2
    kern = functools.partial(_nm_matmul_kernel, n_norm_tiles=norm_cols // tn, head_scale=head_scale)
    return pl.pallas_call(
        kern,
        grid=(T // tm, N // tn),
        in_specs=[
            pl.BlockSpec((tm, D), lambda i, j: (i, 0)),
            pl.BlockSpec((1, D), lambda i, j: (0, 0)),
            pl.BlockSpec((None, 1, D), lambda i, j: (i // rows_per_b, 0, 0)),
            pl.BlockSpec((None, 1, D), lambda i, j: (i // rows_per_b, 0, 0)),
            pl.BlockSpec((D, tn), lambda i, j: (0, j)),
            pl.BlockSpec((1, FOX_DH), lambda i, j: (0, 0)),
        ],
        out_specs=pl.BlockSpec((tm, tn), lambda i, j: (i, j)),
        out_shape=jax.ShapeDtypeStruct((T, N), BF16),
        scratch_shapes=[pltpu.VMEM((tm, D), BF16)],
        compiler_params=_params(("arbitrary", "arbitrary"), vmem),
        name="norm_mod_proj",
    )(h, gain.reshape(1, D), scale, shift, w, head_gain.reshape(1, FOX_DH))


def _proj_residual_kernel(y_ref, w_ref, h_ref, gate_ref, o_ref):
    acc = jnp.dot(y_ref[...], w_ref[...], preferred_element_type=F32)
    o_ref[...] = h_ref[...] + gate_ref[...] * acc


def _proj_residual(y, w, h, gate, seq, *, tm=1024, tn=512):
    T, K = y.shape
    D = w.shape[1]
    tm, tn = _tile(seq, tm), _tile(D, tn)
    rows_per_b = seq // tm
    vmem = 2 * tm * K * 2 + 2 * K * tn * 2 + 4 * tm * tn * 4
    return pl.pallas_call(
        _proj_residual_kernel,
        grid=(T // tm, D // tn),
        in_specs=[
            pl.BlockSpec((tm, K), lambda i, j: (i, 0)),
            pl.BlockSpec((K, tn), lambda i, j: (0, j)),
            pl.BlockSpec((tm, tn), lambda i, j: (i, j)),
            pl.BlockSpec((None, 1, tn), lambda i, j: (i // rows_per_b, 0, j)),
        ],
        out_specs=pl.BlockSpec((tm, tn), lambda i, j: (i, j)),
        out_shape=jax.ShapeDtypeStruct((T, D), F32),
        compiler_params=_params(("arbitrary", "arbitrary"), vmem),
        name="proj_residual",
    )(y, w, h, gate)


def _rotary(x, cos, sin):
    half = x.shape[-1] // 2
    x1, x2 = x[:, :half], x[:, half:]
    return jnp.concatenate([x1 * cos - x2 * sin, x1 * sin + x2 * cos], axis=-1)


def _retention_kernel(lg_ref, q_ref, k_ref, v_ref, g_ref, cos_ref, sin_ref, o_ref, state_ref,
                      *, chunk, n_chunks):
    @pl.when(pl.program_id(2) == 0)
    def _():
        state_ref[...] = jnp.zeros_like(state_ref)

    lg = lg_ref[pl.program_id(1)]
    row = lax.broadcasted_iota(jnp.int32, (chunk, chunk), 0)
    col = lax.broadcasted_iota(jnp.int32, (chunk, chunk), 1)
    dist = (row - col).astype(F32)
    d_local = jnp.where(dist >= 0, jnp.exp(lg * jnp.maximum(dist, 0.0)), 0.0)
    t = lax.broadcasted_iota(jnp.int32, (chunk, 1), 0).astype(F32)
    xi = jnp.exp(lg * (t + 1.0))
    zeta = jnp.exp(lg * (chunk - 1.0 - t))
    gamma_c = jnp.exp(jnp.full((1, 1), lg * chunk, F32))

    def body(ci, carry):
        r0 = pl.multiple_of(ci * chunk, chunk)
        cos = cos_ref[pl.ds(r0, chunk), :]
        sin = sin_ref[pl.ds(r0, chunk), :]
        q = _rotary(q_ref[pl.ds(r0, chunk), :].astype(F32), cos, sin).astype(BF16)
        k = (_rotary(k_ref[pl.ds(r0, chunk), :].astype(F32), cos, sin) * (RET_DK ** -0.5)).astype(BF16)
        v = v_ref[pl.ds(r0, chunk), :]
        scores = lax.dot_general(q, k, (((1,), (1,)), ((), ())), preferred_element_type=F32) * d_local
        inner = jnp.dot(scores.astype(BF16), v, preferred_element_type=F32)
        state = state_ref[...]
        cross = jnp.dot(q, state.astype(BF16), preferred_element_type=F32) * xi
        vz = (v.astype(F32) * zeta).astype(BF16)
        state_ref[...] = gamma_c * state + lax.dot_general(
            k, vz, (((0,), (0,)), ((), ())), preferred_element_type=F32)
        o = inner + cross
        ms = jnp.mean(o * o, axis=-1, keepdims=True)
        g = g_ref[pl.ds(r0, chunk), :].astype(F32)
        o_ref[pl.ds(r0, chunk), :] = (g * jax.nn.sigmoid(g) * (o * lax.rsqrt(ms + EPS))).astype(o_ref.dtype)
        return carry

    lax.fori_loop(0, n_chunks, body, 0)


def _retention(proj, batch, seq, heads, *, chunk=256, rows=1024):
    T = proj.shape[0]
    chunk = _tile(seq, chunk)
    rows = _tile(seq, rows)
    nblk = seq // rows
    half = RET_DK // 2
    inv = 1.0 / (ROPE_BASE ** (jnp.arange(half, dtype=F32) / half))
    ang = jnp.arange(seq, dtype=F32)[:, None] * inv[None, :]
    cos, sin = jnp.cos(ang), jnp.sin(ang)
    log_gamma = jnp.log1p(-jnp.exp2(-5.0 - jnp.arange(heads, dtype=F32)))
    kern = functools.partial(_retention_kernel, chunk=chunk, n_chunks=rows // chunk)
    qk_blocks = heads
    v_off = 2 * heads * RET_DK // RET_DV
    return pl.pallas_call(
        kern,
        grid=(batch, heads, nblk),
        in_specs=[
            pl.BlockSpec(memory_space=pltpu.SMEM),
            pl.BlockSpec((rows, RET_DK), lambda b, h, c: (b * nblk + c, h)),
            pl.BlockSpec((rows, RET_DK), lambda b, h, c: (b * nblk + c, qk_blocks + h)),
            pl.BlockSpec((rows, RET_DV), lambda b, h, c: (b * nblk + c, v_off + h)),
            pl.BlockSpec((rows, RET_DV), lambda b, h, c: (b * nblk + c, v_off + heads + h)),
            pl.BlockSpec((rows, half), lambda b, h, c: (c, 0)),
            pl.BlockSpec((rows, half), lambda b, h, c: (c, 0)),
        ],
        out_specs=pl.BlockSpec((rows, RET_DV), lambda b, h, c: (b * nblk + c, h)),
        out_shape=jax.ShapeDtypeStruct((T, heads * RET_DV), BF16),
        scratch_shapes=[pltpu.VMEM((RET_DK, RET_DV), F32)],
        compiler_params=_params(("arbitrary", "arbitrary", "arbitrary"), 16 * 1024 * 1024),
        name="retention",
    )(log_gamma, proj, proj, proj, proj, cos, sin)


def _small_proj(h_ref, gain_ref, scale_ref, shift_ref, w_ref):
    u = _norm_mod(h_ref[...], gain_ref[...], scale_ref[...], shift_ref[...])
    return u, jnp.dot(u, w_ref[...], precision=HIGHEST, preferred_element_type=F32)


def _forget_kernel(h_ref, gain_ref, scale_ref, shift_ref, w_ref, bias_ref, o_ref, carry_ref,
                   *, tiles_per_seq, n_heads):
    @pl.when(pl.program_id(0) % tiles_per_seq == 0)
    def _():
        carry_ref[...] = jnp.zeros_like(carry_ref)

    _, f = _small_proj(h_ref, gain_ref, scale_ref, shift_ref, w_ref)
    log_f = jax.nn.log_sigmoid(f + bias_ref[...])
    tm = log_f.shape[0]
    row = lax.broadcasted_iota(jnp.int32, (tm, tm), 0)
    col = lax.broadcasted_iota(jnp.int32, (tm, tm), 1)
    tri = jnp.where(col <= row, 1.0, 0.0).astype(F32)
    cum = jnp.dot(tri, log_f, precision=HIGHEST, preferred_element_type=F32) + carry_ref[...]
    carry_ref[...] = cum[tm - 1:tm, :]
    neg = cum * (-LOG2E)
    lane = lax.broadcasted_iota(jnp.int32, (tm, FOX_DH), 1)
    for hh in range(n_heads):
        b = neg[:, hh:hh + 1]
        hi = b.astype(BF16).astype(F32)
        mid = (b - hi).astype(BF16).astype(F32)
        lo = b - hi - mid
        blk = jnp.where(lane == 0, hi, jnp.where(lane == 1, mid, jnp.where(lane == 2, lo, 0.0)))
        o_ref[:, hh * FOX_DH:(hh + 1) * FOX_DH] = blk.astype(o_ref.dtype)


def _forget_cumsum(h, gain, scale, shift, w_f, bias, seq, *, tm=512):
    T, D = h.shape
    nh = w_f.shape[1]
    tm = _tile(seq, tm)
    rows_per_b = seq // tm
    w_pad = jnp.zeros((D, LANES), F32).at[:, :nh].set(w_f)
    b_pad = jnp.zeros((1, LANES), F32).at[0, :nh].set(bias)
    kern = functools.partial(_forget_kernel, tiles_per_seq=rows_per_b, n_heads=nh)
    return pl.pallas_call(
        kern,
        grid=(T // tm,),
        in_specs=[
            pl.BlockSpec((tm, D), lambda i: (i, 0)),
            pl.BlockSpec((1, D), lambda i: (0, 0)),
            pl.BlockSpec((None, 1, D), lambda i: (i // rows_per_b, 0, 0)),
            pl.BlockSpec((None, 1, D), lambda i: (i // rows_per_b, 0, 0)),
            pl.BlockSpec((D, LANES), lambda i: (0, 0)),
            pl.BlockSpec((1, LANES), lambda i: (0, 0)),
        ],
        out_specs=pl.BlockSpec((tm, nh * FOX_DH), lambda i: (i, 0)),
        out_shape=jax.ShapeDtypeStruct((T, nh * FOX_DH), BF16),
        scratch_shapes=[pltpu.VMEM((1, LANES), F32)],
        compiler_params=_params(("arbitrary",), 2 * tm * D * 4 + 4 * tm * D * 4),
        name="forget_cumsum",
    )(h, gain.reshape(1, D), scale, shift, w_pad, b_pad)


def _router_kernel(h_ref, gain_ref, scale_ref, shift_ref, w_ref, u_ref, route_ref, w0_ref, w1_ref,
                   count_ref, carry_ref, *, n_experts):
    @pl.when(pl.program_id(0) == 0)
    def _():
        carry_ref[...] = jnp.zeros_like(carry_ref)

    u, logits = _small_proj(h_ref, gain_ref, scale_ref, shift_ref, w_ref)
    u_ref[...] = u.reshape(u_ref.shape).astype(u_ref.dtype)
    tm = logits.shape[0]
    lane = lax.broadcasted_iota(jnp.int32, (tm, LANES), 1)
    neg = -jnp.inf
    l0 = jnp.where(lane < n_experts, logits, neg)
    m0 = jnp.max(l0, axis=-1, keepdims=True)
    i0 = jnp.min(jnp.where(l0 == m0, lane, LANES), axis=-1, keepdims=True)
    l1 = jnp.where(lane == i0, neg, l0)
    m1 = jnp.max(l1, axis=-1, keepdims=True)
    i1 = jnp.min(jnp.where(l1 == m1, lane, LANES), axis=-1, keepdims=True)
    e = jnp.exp(m1 - m0)
    wgt0 = 1.0 / (1.0 + e)
    wgt1 = e / (1.0 + e)
    onehot = jnp.where(lane == i0, 1.0, jnp.where(lane == i1, 1.0, 0.0))
    row = lax.broadcasted_iota(jnp.int32, (tm, tm), 0)
    col = lax.broadcasted_iota(jnp.int32, (tm, tm), 1)
    tri = jnp.where(col < row, 1.0, 0.0).astype(BF16)
    before = jnp.dot(tri, onehot.astype(BF16), preferred_element_type=F32) + carry_ref[...]
    rank0 = jnp.sum(jnp.where(lane == i0, before, 0.0), axis=-1, keepdims=True)
    rank1 = jnp.sum(jnp.where(lane == i1, before, 0.0), axis=-1, keepdims=True)
    total = carry_ref[...] + jnp.sum(onehot, axis=0, keepdims=True)
    carry_ref[...] = total
    count_ref[...] = jnp.broadcast_to(total, count_ref.shape)
    route = jnp.where(lane == 0, i0.astype(F32),
                      jnp.where(lane == 1, i1.astype(F32),
                                jnp.where(lane == 2, rank0, jnp.where(lane == 3, rank1, 0.0))))
    route_ref[...] = route
    w0_ref[...] = jnp.broadcast_to(wgt0, w0_ref.shape)
    w1_ref[...] = jnp.broadcast_to(wgt1, w1_ref.shape)


def _router(h, gain, scale, shift, w_router, seq, *, tm=512):
    T, D = h.shape
    E = w_router.shape[1]
    tm = _tile(seq, tm)
    rows_per_b = seq // tm
    w_pad = jnp.zeros((D, LANES), F32).at[:, :E].set(w_router)
    kern = functools.partial(_router_kernel, n_experts=E)
    row_spec = pl.BlockSpec((tm, LANES), lambda i: (i, 0))
    row_shape = jax.ShapeDtypeStruct((T, LANES), F32)
    return pl.pallas_call(
        kern,
        grid=(T // tm,),
        in_specs=[
            pl.BlockSpec((tm, D), lambda i: (i, 0)),
            pl.BlockSpec((1, D), lambda i: (0, 0)),
            pl.BlockSpec((None, 1, D), lambda i: (i // rows_per_b, 0, 0)),
            pl.BlockSpec((None, 1, D), lambda i: (i // rows_per_b, 0, 0)),
            pl.BlockSpec((D, LANES), lambda i: (0, 0)),
        ],
        out_specs=[
            pl.BlockSpec((tm,) + _row_tile(D), lambda i: (i, 0, 0)),
            row_spec, row_spec, row_spec,
            pl.BlockSpec((8, LANES), lambda i: (0, 0)),
        ],
        out_shape=[
            jax.ShapeDtypeStruct((T,) + _row_tile(D), BF16),
            row_shape, row_shape, row_shape,
            jax.ShapeDtypeStruct((8, LANES), F32),
        ],
        scratch_shapes=[pltpu.VMEM((1, LANES), F32)],
        compiler_params=_params(("arbitrary",), 2 * tm * D * 4 + 2 * tm * D * 2 + 4 * tm * D * 4),
        name="moe_router",
    )(h, gain.reshape(1, D), scale, shift, w_pad)


def _fox_kernel(ti_ref, tj_ref, q_ref, k_ref, e_ref, v_ref, o_ref, kp_ref, vp_ref, m_ref, acc_ref,
                *bufs, tq, n_off, n_diag):
    lane = lax.broadcasted_iota(jnp.int32, k_ref.shape, 1)
    kp_ref[:FOX_DH, :] = k_ref[...].T
    kp_ref[FOX_DH:, :] = e_ref[...].T
    vp_ref[:, :FOX_DH] = v_ref[...]
    vp_ref[:, FOX_DH:] = jnp.where(lane == 0, 1.0, 0.0).astype(BF16)
    m_ref[...] = jnp.full(m_ref.shape, -jnp.inf, F32)
    acc_ref[...] = jnp.zeros_like(acc_ref)

    LAG = FOX_PIPELINE_LAG
    nbuf = 2 * LAG
    lane = lax.broadcasted_iota(jnp.int32, (tq, FOX_DH), 1)
    ones = jnp.where(lane < 3, 1.0, 0.0).astype(BF16)
    s_refs, p_refs, a_refs = bufs[:nbuf], bufs[nbuf:2 * nbuf], bufs[2 * nbuf:]

    def tile_rows(idx_ref, t):
        return pl.ds(pl.multiple_of(idx_ref[t] * tq, tq), tq)

    def logits(t, par):
        qp = jnp.concatenate([q_ref[tile_rows(ti_ref, t), :], ones], axis=-1)
        kp = kp_ref[:, tile_rows(tj_ref, t)]
        s_refs[par][...] = jnp.dot(qp, kp, preferred_element_type=F32)

    def softmax(t, par, diag):
        rows = tile_rows(ti_ref, t)
        s = s_refs[par][...]
        if diag:
            row = lax.broadcasted_iota(jnp.int32, (tq, tq), 0)
            col = lax.broadcasted_iota(jnp.int32, (tq, tq), 1)
            s = jnp.where(col <= row, s, -jnp.inf)
        m = m_ref[rows, :]
        m_new = jnp.maximum(m, jnp.max(s, axis=-1, keepdims=True))
        m_ref[rows, :] = m_new
        a_refs[par][...] = jnp.exp2(m - m_new)
        p_refs[par][...] = jnp.exp2(s - m_new).astype(BF16)

    def values(t, par, diag):
        rows = tile_rows(ti_ref, t)
        vp = vp_ref[tile_rows(tj_ref, t), :]
        acc = a_refs[par][...] * acc_ref[rows, :] + jnp.dot(p_refs[par][...], vp,
                                                            preferred_element_type=F32)
        if diag:
            o_ref[rows, :] = (acc[:, :FOX_DH] / acc[:, FOX_DH:FOX_DH + 1]).astype(o_ref.dtype)
        else:
            acc_ref[rows, :] = acc

    n = n_off + n_diag
    last = n - 1 + 2 * LAG

    def plan(u):
        tv, ts = u - 2 * LAG, u - LAG
        return (0 <= tv < n, tv >= n_off), (0 <= ts < n, ts >= n_off), u < n

    def step(u, slot, stages):
        (do_v, diag_v), (do_s, diag_s), do_l = stages
        if do_v:
            values(u - 2 * LAG, slot, diag_v)
        if do_s:
            softmax(u - LAG, (slot + LAG) % nbuf, diag_s)
        if do_l:
            logits(u, slot)

    def uniform(u0):
        plans = {plan(u0 + g) for g in range(nbuf)}
        stages = next(iter(plans))
        full = stages[0][0] and stages[1][0] and stages[2]
        return stages if len(plans) == 1 and full and u0 + nbuf - 1 <= last else None

    u = 0
    while u <= last:
        stages = uniform(u) if u % nbuf == 0 else None
        if stages is None:
            step(u, u % nbuf, plan(u))
            u += 1
            continue
        count = 1
        while uniform(u + count * nbuf) == stages:
            count += 1

        def group(d, c, base=u, stages=stages):
            for g in range(nbuf):
                step(base + nbuf * d + g, g, stages)
            return c

        lax.fori_loop(0, count, group, 0)
        u += count * nbuf


def _fox_attention(q, kv, e, batch, seq, heads, *, tq=512):
    T = q.shape[0]
    tq = _tile(seq, tq)
    nq = seq // tq
    below = [(i, j) for i in range(nq) for j in range(i)]
    tiles = below + [(i, i) for i in range(nq)]
    ti = jnp.array([t[0] for t in tiles], jnp.int32)
    tj = jnp.array([t[1] for t in tiles], jnp.int32)
    kern = functools.partial(_fox_kernel, tq=tq, n_off=len(below), n_diag=nq)
    nbuf = 2 * FOX_PIPELINE_LAG
    grid_spec = pltpu.PrefetchScalarGridSpec(
        num_scalar_prefetch=2,
        grid=(batch, heads),
        in_specs=[
            pl.BlockSpec((seq, FOX_DH), lambda b, h, ti, tj: (b, h)),
            pl.BlockSpec((seq, FOX_DH), lambda b, h, ti, tj: (b, h)),
            pl.BlockSpec((seq, FOX_DH), lambda b, h, ti, tj: (b, h)),
            pl.BlockSpec((seq, FOX_DH), lambda b, h, ti, tj: (b, heads + h)),
        ],
        out_specs=pl.BlockSpec((seq, FOX_DH), lambda b, h, ti, tj: (b, h)),
        scratch_shapes=([pltpu.VMEM((2 * FOX_DH, seq), BF16), pltpu.VMEM((seq, 2 * FOX_DH), BF16),
                         pltpu.VMEM((seq, 1), F32), pltpu.VMEM((seq, 2 * FOX_DH), F32)]
                        + [pltpu.VMEM((tq, tq), F32)] * nbuf + [pltpu.VMEM((tq, tq), BF16)] * nbuf
                        + [pltpu.VMEM((tq, 1), F32)] * nbuf),
    )
    vmem = 14 * seq * FOX_DH * 2 + seq * (LANES + 2 * FOX_DH) * 4 + nbuf * tq * (6 * tq + 4 * LANES)
    return pl.pallas_call(
        kern,
        grid_spec=grid_spec,
        out_shape=jax.ShapeDtypeStruct((T, heads * FOX_DH), BF16),
        compiler_params=_params(("arbitrary", "arbitrary"), vmem),
        name="fox_attention",
    )(ti, tj, q, kv, e, kv)


def _swiglu_step(u, w1_ref, w3_ref, w2_ref, acc_ref):
    a = jnp.dot(u, w1_ref[...].astype(BF16), preferred_element_type=F32)
    g = jnp.dot(u, w3_ref[...].astype(BF16), preferred_element_type=F32)
    mid = (a * jax.nn.sigmoid(a) * g).astype(BF16)
    acc_ref[...] += jnp.dot(mid, w2_ref[...].astype(BF16), preferred_element_type=F32)


def _dense_ffn_kernel(h_ref, gain_ref, scale_ref, shift_ref, gate_ref, w1_ref, w3_ref, w2_ref,
                      o_ref, u_ref, acc_ref):
    f = pl.program_id(1)

    @pl.when(f == 0)
    def _():
        u_ref[...] = _norm_mod(h_ref[...], gain_ref[...], scale_ref[...], shift_ref[...]).astype(BF16)
        acc_ref[...] = jnp.zeros_like(acc_ref)

    _swiglu_step(u_ref[...], w1_ref, w3_ref, w2_ref, acc_ref)

    @pl.when(f == pl.num_programs(1) - 1)
    def _():
        o_ref[...] = h_ref[...] + gate_ref[...] * acc_ref[...]


def _dense_ffn(h, gain, scale, shift, gate, w13, w2, seq, *, tm=512, tf=512):
    T, D = h.shape
    F = w2.shape[0]
    tm, tf = _tile(seq, tm), _tile(F, tf)
    nf = F // tf
    rows_per_b = seq // tm
    vec = pl.BlockSpec((None, 1, D), lambda i, f: (i // rows_per_b, 0, 0))
    vmem = 4 * tm * D * 4 + tm * D * 2 + tm * D * 4 + 2 * 3 * D * tf * 2
    return pl.pallas_call(
        _dense_ffn_kernel,
        grid=(T // tm, nf),
        in_specs=[
            pl.BlockSpec((tm, D), lambda i, f: (i, 0)),
            pl.BlockSpec((1, D), lambda i, f: (0, 0)),
            vec, vec, vec,
            pl.BlockSpec((D, tf), lambda i, f: (0, f)),
            pl.BlockSpec((D, tf), lambda i, f: (0, nf + f)),
            pl.BlockSpec((tf, D), lambda i, f: (f, 0)),
        ],
        out_specs=pl.BlockSpec((tm, D), lambda i, f: (i, 0)),
        out_shape=jax.ShapeDtypeStruct((T, D), F32),
        scratch_shapes=[pltpu.VMEM((tm, D), BF16), pltpu.VMEM((tm, D), F32)],
        compiler_params=_params(("arbitrary", "arbitrary"), vmem),
        name="dense_swiglu",
    )(h, gain.reshape(1, D), scale, shift, gate, w13, w13, w2)


def _expert_ffn_kernel(te_ref, nu_ref, x_ref, w1_ref, w3_ref, w2_ref, o_ref, x2_ref, acc_ref):
    i, f = pl.program_id(0), pl.program_id(1)
    used = i < nu_ref[0]

    @pl.when(f == 0)
    def _():
        x2_ref[...] = x_ref[...].astype(F32).reshape(x2_ref.shape).astype(x2_ref.dtype)
        acc_ref[...] = jnp.zeros_like(acc_ref)

    @pl.when(used)
    def _():
        _swiglu_step(x2_ref[...], w1_ref, w3_ref, w2_ref, acc_ref)

    @pl.when(f == pl.num_programs(1) - 1)
    def _():
        o_ref[...] = acc_ref[...].reshape(o_ref.shape).astype(o_ref.dtype)


def _expert_ffn(xs, tile_expert, n_used, w13, w2, layer, *, tm, tf=256):
    P = xs.shape[0]
    row_tile = xs.shape[1:]
    _, E, F, D = w2.shape
    tf = _tile(F, tf)
    nf = F // tf

    def fidx(i, f, nu):
        return jnp.where(i < nu[0], f, nf - 1)

    grid_spec = pltpu.PrefetchScalarGridSpec(
        num_scalar_prefetch=2,
        grid=(P // tm, nf),
        in_specs=[
            pl.BlockSpec((tm,) + row_tile, lambda i, f, te, nu: (i, 0, 0)),
            pl.BlockSpec((None, None, D, tf), lambda i, f, te, nu: (layer, te[i], 0, fidx(i, f, nu))),
            pl.BlockSpec((None, None, D, tf), lambda i, f, te, nu: (layer, te[i], 0, nf + fidx(i, f, nu))),
            pl.BlockSpec((None, None, tf, D), lambda i, f, te, nu: (layer, te[i], fidx(i, f, nu), 0)),
        ],
        out_specs=pl.BlockSpec((tm,) + row_tile, lambda i, f, te, nu: (i, 0, 0)),
        scratch_shapes=[pltpu.VMEM((tm, D), BF16), pltpu.VMEM((tm, D), F32)],
    )
    w_bytes = jnp.dtype(w2.dtype).itemsize
    vmem = 5 * tm * D * 2 + tm * D * 4 + 3 * D * tf * (2 * w_bytes + 2) + 4 * tm * tf * 4
    return pl.pallas_call(
        _expert_ffn_kernel,
        grid_spec=grid_spec,
        out_shape=jax.ShapeDtypeStruct((P,) + row_tile, BF16),
        compiler_params=_params(("arbitrary", "arbitrary"), vmem),
        name="expert_swiglu",
    )(tile_expert, n_used, xs, w13, w13, w2)


def _dispatch_kernel(pos_ref, u_ref, xs_in_ref, xs_ref, sem, *, chunk):
    del xs_in_ref

    def copy(r):
        return pltpu.make_async_copy(u_ref.at[r // TOP_K], xs_ref.at[pos_ref[r]], sem)

    def start(r, c):
        copy(r).start()
        return c

    def wait(r, c):
        copy(r).wait()
        return c

    lax.fori_loop(0, TOP_K * chunk, start, 0, unroll=8)
    lax.fori_loop(0, TOP_K * chunk, wait, 0, unroll=8)


def _dispatch(u3, pos_flat, n_slots, *, chunk=512):
    T = u3.shape[0]
    chunk = _tile(T, chunk)
    xs0 = jnp.zeros((n_slots,) + u3.shape[1:], u3.dtype)
    kern = functools.partial(_dispatch_kernel, chunk=chunk)
    return pl.pallas_call(
        kern,
        grid=(T // chunk,),
        in_specs=[
            pl.BlockSpec((TOP_K * chunk,), lambda i: (i,), memory_space=pltpu.SMEM),
            pl.BlockSpec((chunk,) + u3.shape[1:], lambda i: (i, 0, 0)),
            pl.BlockSpec(memory_space=pl.ANY),
        ],
        out_specs=pl.BlockSpec(memory_space=pl.ANY),
        out_shape=jax.ShapeDtypeStruct(xs0.shape, xs0.dtype),
        scratch_shapes=[pltpu.SemaphoreType.DMA(())],
        input_output_aliases={2: 0},
        compiler_params=_params(("arbitrary",), 4 * chunk * u3.shape[1] * u3.shape[2] * 2),
        name="moe_dispatch",
    )(pos_flat, u3, xs0)


def _combine_kernel(pos_ref, ys_ref, w0_ref, w1_ref, h_ref, gate_ref, o_ref, buf_ref, sem, *, chunk):
    def copy(r):
        return pltpu.make_async_copy(ys_ref.at[pos_ref[r]], buf_ref.at[r % TOP_K, r // TOP_K], sem)

    def start(r, c):
        copy(r).start()
        return c

    def wait(r, c):
        copy(r).wait()
        return c

    lax.fori_loop(0, TOP_K * chunk, start, 0, unroll=8)
    lax.fori_loop(0, TOP_K * chunk, wait, 0, unroll=8)
    y = w0_ref[...] * buf_ref[0].astype(F32) + w1_ref[...] * buf_ref[1].astype(F32)
    o_ref[...] = h_ref[...] + gate_ref[...] * y.reshape(o_ref.shape)


def _combine(ys3, pos_flat, w0, w1, h, gate, seq, *, chunk=512):
    T, D = h.shape
    tile = ys3.shape[1:]
    chunk = _tile(seq, chunk)
    rows_per_b = seq // chunk
    kern = functools.partial(_combine_kernel, chunk=chunk)
    wspec = pl.BlockSpec((chunk, 1, LANES), lambda i: (i, 0, 0))
    return pl.pallas_call(
        kern,
        grid=(T // chunk,),
        in_specs=[
            pl.BlockSpec((TOP_K * chunk,), lambda i: (i,), memory_space=pltpu.SMEM),
            pl.BlockSpec(memory_space=pl.ANY),
            wspec, wspec,
            pl.BlockSpec((chunk, D), lambda i: (i, 0)),
            pl.BlockSpec((None, 1, D), lambda i: (i // rows_per_b, 0, 0)),
        ],
        out_specs=pl.BlockSpec((chunk, D), lambda i: (i, 0)),
        out_shape=jax.ShapeDtypeStruct((T, D), F32),
        scratch_shapes=[pltpu.VMEM((TOP_K, chunk) + tile, BF16), pltpu.SemaphoreType.DMA(())],
        compiler_params=_params(("arbitrary",), 10 * chunk * D * 4),
        name="moe_combine",
    )(pos_flat, ys3, w0.reshape(T, 1, LANES), w1.reshape(T, 1, LANES), h, gate)


def _moe_layer(h, gain, scale, shift, gate, w_router, w13, w2, layer, seq, *, tm=1024):
    T, D = h.shape
    E = w_router.shape[1]
    u3, route, w0, w1, counts = _router(h, gain, scale, shift, w_router, seq)

    expert = route[:, :TOP_K].astype(jnp.int32)
    rank = route[:, TOP_K:2 * TOP_K].astype(jnp.int32)
    count = counts[0, :E].astype(jnp.int32)
    padded = ((count + tm - 1) // tm) * tm
    ends = jnp.cumsum(padded)
    starts = ends - padded
    pos_flat = (starts[expert] + rank).reshape(-1)
    n_tiles = (TOP_K * T) // tm + E
    tile_start = jnp.arange(n_tiles, dtype=jnp.int32) * tm
    n_used = (ends[-1] // tm).astype(jnp.int32).reshape(1)
    tile_expert = jnp.minimum(jnp.searchsorted(ends, tile_start, side="right"), E - 1).astype(jnp.int32)
    tile_expert = jnp.where(tile_start < ends[-1], tile_expert, tile_expert[jnp.maximum(n_used[0] - 1, 0)])

    xs3 = _dispatch(u3, pos_flat, n_tiles * tm)
    ys3 = _expert_ffn(xs3, tile_expert, n_used, w13, w2, layer, tm=tm)
    return _combine(ys3, pos_flat, w0, w1, h, gate, seq)


def kernel(x, c, norm_gain, mod_w, mod_b, ret_w_in, ret_w_out, fox_w_q, fox_q_gain, fox_w_out,
           kv_norm_gain, kv_mod_w, kv_mod_b, kv_w, kv_forget_bias, kv_k_gain,
           ffn_w13, ffn_w2, moe_router, moe_w13, moe_w2):
    B, S, D = x.shape
    T = B * S
    depth = mod_w.shape[0]
    n_ret = ret_w_in.shape[0]
    ret_heads = ret_w_out.shape[1] // RET_DV
    fox_heads = fox_w_q.shape[2] // FOX_DH
    fox_width = fox_heads * FOX_DH

    c8 = jnp.zeros((8, D), F32).at[:B].set(c)
    mod = _mod_call(c8, mod_w, mod_b)[:, :B]
    kv_mod = _mod_call(c8, kv_mod_w[None], kv_mod_b[None])[0, :B]

    def vecs(m, n):
        return [v.reshape(B, 1, D) for v in jnp.split(m, n, axis=-1)]

    h = x.reshape(T, D)
    kv_sh = decay_sh = None
    for i in range(depth):
        sh1, sc1, g1, sh2, sc2, g2 = vecs(mod[i], 6)
        if i < n_ret:
            proj = _nm_matmul(h, norm_gain[i, 0], sc1, sh1, ret_w_in[i].astype(BF16), S)
            y = _retention(proj, B, S, ret_heads)
            h = _proj_residual(y, ret_w_out[i].astype(BF16), h, g1, S)
        else:
            j = i - n_ret
            q = _nm_matmul(h, norm_gain[i, 0], sc1, sh1, fox_w_q[j].astype(BF16), S,
                           head_gain=fox_q_gain[j], norm_cols=fox_width, head_scale=LOG2E * FOX_DH ** -0.5)
            y = _fox_attention(q, kv_sh, decay_sh, B, S, fox_heads)
            h = _proj_residual(y, fox_w_out[j].astype(BF16), h, g1, S, tn=1024)
        if i % 2 == 0:
            h = _dense_ffn(h, norm_gain[i, 1], sc2, sh2, g2,
                           ffn_w13[i // 2].astype(BF16), ffn_w2[i // 2].astype(BF16), S)
        else:
            h = _moe_layer(h, norm_gain[i, 1], sc2, sh2, g2, moe_router[i // 2], moe_w13, moe_w2, i // 2, S)
        if i == n_ret - 1:
            ksh, ksc = vecs(kv_mod, 2)
            kv_sh = _nm_matmul(h, kv_norm_gain, ksc, ksh, kv_w[:, :2 * fox_width].astype(BF16), S,
                               head_gain=kv_k_gain, norm_cols=fox_width)
            decay_sh = _forget_cumsum(h, kv_norm_gain, ksc, ksh, kv_w[:, 2 * fox_width:], kv_forget_bias, S)
    return h.reshape(B, S, D)
```

```python
import functools
import math

import jax
import jax.numpy as jnp
from jax import lax
from jax.experimental import pallas as pl
from jax.experimental.pallas import tpu as pltpu

F32 = jnp.float32
BF16 = jnp.bfloat16
HIGHEST = lax.Precision.HIGHEST

EPS = 1e-6
ROPE_BASE = 10000.0
RET_DK = 256
RET_DV = 2 * RET_DK
FOX_DH = 128
FOX_PIPELINE_LAG = 2
TOP_K = 2
LOG2E = math.log2(math.e)
LANES = 128
SUBLANES_BF16 = 16
VMEM_LIMIT_CAP = 56 * 1024 * 1024


def _params(semantics, vmem_bytes):
    limit = int(min(max(2 * vmem_bytes, 16 * 1024 * 1024), VMEM_LIMIT_CAP))
    return pltpu.CompilerParams(dimension_semantics=semantics, vmem_limit_bytes=limit)


def _tile(n, pref):
    t = min(n, pref)
    assert n % t == 0, (n, pref)
    return t


def _row_tile(d):
    assert d % (SUBLANES_BF16 * LANES) == 0, d
    return (d // LANES, LANES)


def _norm_mod(h, gain, scale, shift):
    ms = jnp.mean(h * h, axis=-1, keepdims=True)
    return h * lax.rsqrt(ms + EPS) * gain * (1.0 + scale) + shift


def _mod_kernel(c_ref, w_ref, b_ref, o_ref):
    c = c_ref[...]
    cond = c * jax.nn.sigmoid(c)
    o_ref[...] = jnp.dot(cond, w_ref[...], precision=HIGHEST, preferred_element_type=F32) + b_ref[...]


def _mod_call(c8, w, b):
    L, D, N = w.shape
    tn = _tile(N, 1024)
    return pl.pallas_call(
        _mod_kernel,
        grid=(L, N // tn),
        in_specs=[
            pl.BlockSpec((8, D), lambda l, j: (0, 0)),
            pl.BlockSpec((None, D, tn), lambda l, j: (l, 0, j)),
            pl.BlockSpec((None, 1, tn), lambda l, j: (l, 0, j)),
        ],
        out_specs=pl.BlockSpec((None, 8, tn), lambda l, j: (l, 0, j)),
        out_shape=jax.ShapeDtypeStruct((L, 8, N), F32),
        compiler_params=_params(("arbitrary", "arbitrary"), 2 * D * tn * 4),
        name="adaln_mod",
    )(c8, w, b.reshape(L, 1, N))


def _nm_matmul_kernel(h_ref, gain_ref, scale_ref, shift_ref, w_ref, aux_a_ref, aux_b_ref, o_ref, u_ref,
                      *, mode, n_special, head_scale):
    j = pl.program_id(1)

    @pl.when(j == 0)
    def _():
        u_ref[...] = _norm_mod(h_ref[...], gain_ref[...], scale_ref[...], shift_ref[...]).astype(BF16)

    acc = jnp.dot(u_ref[...], w_ref[...], preferred_element_type=F32)

    if n_special == 0:
        o_ref[...] = acc.astype(o_ref.dtype)
        return

    @pl.when(j < n_special)
    def _():
        if mode == "head_norm":
            hg = aux_a_ref[...] * head_scale
            for c in range(acc.shape[1] // FOX_DH):
                blk = acc[:, c * FOX_DH:(c + 1) * FOX_DH]
                ms = jnp.mean(blk * blk, axis=-1, keepdims=True)
                o_ref[:, c * FOX_DH:(c + 1) * FOX_DH] = (blk * lax.rsqrt(ms + EPS) * hg).astype(o_ref.dtype)
        else:
            half = RET_DK // 2
            k_scale = jnp.where(j >= n_special // 2, head_scale, 1.0)
            cos, sin = aux_a_ref[...] * k_scale, aux_b_ref[...] * k_scale
            for c in range(acc.shape[1] // RET_DK):
                x1 = acc[:, c * RET_DK:c * RET_DK + half]
                x2 = acc[:, c * RET_DK + half:(c + 1) * RET_DK]
                o_ref[:, c * RET_DK:c * RET_DK + half] = (x1 * cos - x2 * sin).astype(o_ref.dtype)
                o_ref[:, c * RET_DK + half:(c + 1) * RET_DK] = (x1 * sin + x2 * cos).astype(o_ref.dtype)

    @pl.when(j >= n_special)
    def _():
        o_ref[...] = acc.astype(o_ref.dtype)


def _nm_matmul(h, gain, scale, shift, w, seq, *, mode="head_norm", special_cols=0, head_gain=None,
               rope=None, head_scale=1.0, tm=1024, tn=1024):
    T, D = h.shape
    N = w.shape[1]
    tm, tn = _tile(seq, tm), _tile(N, tn)
    assert special_cols % tn == 0
    rows_per_b = seq // tm
    if mode == "rotary":
        aux = rope
        aux_spec = pl.BlockSpec((tm, RET_DK // 2), lambda i, j: (i % rows_per_b, 0))
    else:
        gain_row = (jnp.ones((FOX_DH,), F32) if head_gain is None else head_gain).reshape(1, FOX_DH)
        aux = (gain_row, gain_row)
        aux_spec = pl.BlockSpec((1, FOX_DH), lambda i, j: (0, 0))
    vmem = 2 * tm * D * 4 + 2 * D * tn * 2 + 2 * tm * tn * 2 + tm * D * 2 + 4 * tm * LANES * 4
    kern = functools.partial(_nm_matmul_kernel, mode=mode, n_special=special_cols // tn,
                             head_scale=head_scale)
    return pl.pallas_call(
        kern,
        grid=(T // tm, N // tn),
        in_specs=[
            pl.BlockSpec((tm, D), lambda i, j: (i, 0)),
            pl.BlockSpec((1, D), lambda i, j: (0, 0)),
            pl.BlockSpec((None, 1, D), lambda i, j: (i // rows_per_b, 0, 0)),
            pl.BlockSpec((None, 1, D), lambda i, j: (i // rows_per_b, 0, 0)),
            pl.BlockSpec((D, tn), lambda i, j: (0, j)),
            aux_spec, aux_spec,
        ],
        out_specs=pl.BlockSpec((tm, tn), lambda i, j: (i, j)),
        out_shape=jax.ShapeDtypeStruct((T, N), BF16),
        scratch_shapes=[pltpu.VMEM((tm, D), BF16)],
        compiler_params=_params(("arbitrary", "arbitrary"), vmem),
        name="norm_mod_proj",
    )(h, gain.reshape(1, D), scale, shift, w, *aux)


def _proj_residual_kernel(y_ref, w_ref, h_ref, gate_ref, o_ref):
    acc = jnp.dot(y_ref[...], w_ref[...], preferred_element_type=F32)
    o_ref[...] = h_ref[...] + gate_ref[...] * acc


def _proj_residual(y, w, h, gate, seq, *, tm=1024, tn=512):
    T, K = y.shape
    D = w.shape[1]
    tm, tn = _tile(seq, tm), _tile(D, tn)
    rows_per_b = seq // tm
    vmem = 2 * tm * K * 2 + 2 * K * tn * 2 + 4 * tm * tn * 4
    return pl.pallas_call(
        _proj_residual_kernel,
        grid=(T // tm, D // tn),
        in_specs=[
            pl.BlockSpec((tm, K), lambda i, j: (i, 0)),
            pl.BlockSpec((K, tn), lambda i, j: (0, j)),
            pl.BlockSpec((tm, tn), lambda i, j: (i, j)),
            pl.BlockSpec((None, 1, tn), lambda i, j: (i // rows_per_b, 0, j)),
        ],
        out_specs=pl.BlockSpec((tm, tn), lambda i, j: (i, j)),
        out_shape=jax.ShapeDtypeStruct((T, D), F32),
        compiler_params=_params(("arbitrary", "arbitrary"), vmem),
        name="proj_residual",
    )(y, w, h, gate)


def _retention_kernel(lg_ref, q_ref, k_ref, v_ref, g_ref, o_ref, state_ref, *, chunk, n_chunks):
    @pl.when(pl.program_id(2) == 0)
    def _():
        state_ref[...] = jnp.zeros_like(state_ref)

    lg = lg_ref[pl.program_id(1)]
    row = lax.broadcasted_iota(jnp.int32, (chunk, chunk), 0)
    col = lax.broadcasted_iota(jnp.int32, (chunk, chunk), 1)
    dist = (row - col).astype(F32)
    d_local = jnp.where(dist >= 0, jnp.exp(lg * jnp.maximum(dist, 0.0)), 0.0)
    t = lax.broadcasted_iota(jnp.int32, (chunk, 1), 0).astype(F32)
    xi = jnp.exp(lg * (t + 1.0))
    zeta = jnp.exp(lg * (chunk - 1.0 - t))
    gamma_c = jnp.exp(jnp.full((1, 1), lg * chunk, F32))

    def body(ci, carry):
        r0 = pl.multiple_of(ci * chunk, chunk)
        q = q_ref[pl.ds(r0, chunk), :]
        k = k_ref[pl.ds(r0, chunk), :]
        v = v_ref[pl.ds(r0, chunk), :]
        scores = lax.dot_general(q, k, (((1,), (1,)), ((), ())), preferred_element_type=F32) * d_local
        inner = jnp.dot(scores.astype(BF16), v, preferred_element_type=F32)
        state = state_ref[...]
        cross = jnp.dot(q, state.astype(BF16), preferred_element_type=F32) * xi
        vz = (v.astype(F32) * zeta).astype(BF16)
        state_ref[...] = gamma_c * state + lax.dot_general(
            k, vz, (((0,), (0,)), ((), ())), preferred_element_type=F32)
        o = inner + cross
        ms = jnp.mean(o * o, axis=-1, keepdims=True)
        g = g_ref[pl.ds(r0, chunk), :].astype(F32)
        o_ref[pl.ds(r0, chunk), :] = (g * jax.nn.sigmoid(g) * (o * lax.rsqrt(ms + EPS))).astype(o_ref.dtype)
        return carry

    lax.fori_loop(0, n_chunks, body, 0, unroll=True)


def _retention(proj, batch, seq, heads, *, chunk=256, rows=1024):
    T = proj.shape[0]
    chunk = _tile(seq, chunk)
    rows = _tile(seq, rows)
    nblk = seq // rows
    log_gamma = jnp.log1p(-jnp.exp2(-5.0 - jnp.arange(heads, dtype=F32)))
    kern = functools.partial(_retention_kernel, chunk=chunk, n_chunks=rows // chunk)
    qk_blocks = heads
    v_off = 2 * heads * RET_DK // RET_DV
    return pl.pallas_call(
        kern,
        grid=(batch, heads, nblk),
        in_specs=[
            pl.BlockSpec(memory_space=pltpu.SMEM),
            pl.BlockSpec((rows, RET_DK), lambda b, h, c: (b * nblk + c, h)),
            pl.BlockSpec((rows, RET_DK), lambda b, h, c: (b * nblk + c, qk_blocks + h)),
            pl.BlockSpec((rows, RET_DV), lambda b, h, c: (b * nblk + c, v_off + h)),
            pl.BlockSpec((rows, RET_DV), lambda b, h, c: (b * nblk + c, v_off + heads + h)),
        ],
        out_specs=pl.BlockSpec((rows, RET_DV), lambda b, h, c: (b * nblk + c, h)),
        out_shape=jax.ShapeDtypeStruct((T, heads * RET_DV), BF16),
        scratch_shapes=[pltpu.VMEM((RET_DK, RET_DV), F32)],
        compiler_params=_params(("arbitrary", "arbitrary", "arbitrary"), 16 * 1024 * 1024),
        name="retention",
    )(log_gamma, proj, proj, proj, proj)


def _bf16_pieces(x, n):
    pieces = []
    for _ in range(n):
        p = x.astype(BF16)
        pieces.append(p)
        x = x - p.astype(F32)
    return pieces


def _small_proj(h_ref, gain_ref, scale_ref, shift_ref, w_ref):
    u = _norm_mod(h_ref[...], gain_ref[...], scale_ref[...], shift_ref[...])
    u_hi, u_lo = _bf16_pieces(u, 2)
    w_hi, w_lo = _bf16_pieces(w_ref[...], 2)
    out = (jnp.dot(u_hi, w_hi, preferred_element_type=F32) + jnp.dot(u_hi, w_lo, preferred_element_type=F32)
           + jnp.dot(u_lo, w_hi, preferred_element_type=F32))
    return u, out


def _forget_kernel(h_ref, gain_ref, scale_ref, shift_ref, w_ref, bias_ref, o_ref, carry_ref,
                   *, tiles_per_seq, n_heads):
    @pl.when(pl.program_id(0) % tiles_per_seq == 0)
    def _():
        carry_ref[...] = jnp.zeros_like(carry_ref)

    _, f = _small_proj(h_ref, gain_ref, scale_ref, shift_ref, w_ref)
    log_f = jax.nn.log_sigmoid(f + bias_ref[...])
    tm = log_f.shape[0]
    row = lax.broadcasted_iota(jnp.int32, (tm, tm), 0)
    col = lax.broadcasted_iota(jnp.int32, (tm, tm), 1)
    tri = jnp.where(col <= row, 1.0, 0.0).astype(BF16)
    cum = sum(jnp.dot(tri, piece, preferred_element_type=F32) for piece in _bf16_pieces(log_f, 3))
    cum = cum + carry_ref[...]
    carry_ref[...] = cum[tm - 1:tm, :]
    neg = cum * (-LOG2E)
    lane = lax.broadcasted_iota(jnp.int32, (tm, FOX_DH), 1)
    for hh in range(n_heads):
        b = neg[:, hh:hh + 1]
        hi = b.astype(BF16).astype(F32)
        mid = (b - hi).astype(BF16).astype(F32)
        lo = b - hi - mid
        blk = jnp.where(lane == 0, hi, jnp.where(lane == 1, mid, jnp.where(lane == 2, lo, 0.0)))
        o_ref[:, hh * FOX_DH:(hh + 1) * FOX_DH] = blk.astype(o_ref.dtype)


def _forget_cumsum(h, gain, scale, shift, w_f, bias, seq, *, tm=512):
    T, D = h.shape
    nh = w_f.shape[1]
    tm = _tile(seq, tm)
    rows_per_b = seq // tm
    w_pad = jnp.zeros((D, LANES), F32).at[:, :nh].set(w_f)
    b_pad = jnp.zeros((1, LANES), F32).at[0, :nh].set(bias)
    kern = functools.partial(_forget_kernel, tiles_per_seq=rows_per_b, n_heads=nh)
    return pl.pallas_call(
        kern,
        grid=(T // tm,),
        in_specs=[
            pl.BlockSpec((tm, D), lambda i: (i, 0)),
            pl.BlockSpec((1, D), lambda i: (0, 0)),
            pl.BlockSpec((None, 1, D), lambda i: (i // rows_per_b, 0, 0)),
            pl.BlockSpec((None, 1, D), lambda i: (i // rows_per_b, 0, 0)),
            pl.BlockSpec((D, LANES), lambda i: (0, 0)),
            pl.BlockSpec((1, LANES), lambda i: (0, 0)),
        ],
        out_specs=pl.BlockSpec((tm, nh * FOX_DH), lambda i: (i, 0)),
        out_shape=jax.ShapeDtypeStruct((T, nh * FOX_DH), BF16),
        scratch_shapes=[pltpu.VMEM((1, LANES), F32)],
        compiler_params=_params(("arbitrary",), 2 * tm * D * 4 + 4 * tm * D * 4),
        name="forget_cumsum",
    )(h, gain.reshape(1, D), scale, shift, w_pad, b_pad)


def _router_kernel(h_ref, gain_ref, scale_ref, shift_ref, w_ref, u_ref, route_ref, w0_ref, w1_ref,
                   count_ref, carry_ref, *, n_experts):
    @pl.when(pl.program_id(0) == 0)
    def _():
        carry_ref[...] = jnp.zeros_like(carry_ref)

    u, logits = _small_proj(h_ref, gain_ref, scale_ref, shift_ref, w_ref)
    u_ref[...] = u.reshape(u_ref.shape).astype(u_ref.dtype)
    tm = logits.shape[0]
    lane = lax.broadcasted_iota(jnp.int32, (tm, LANES), 1)
    neg = -jnp.inf
    l0 = jnp.where(lane < n_experts, logits, neg)
    m0 = jnp.max(l0, axis=-1, keepdims=True)
    i0 = jnp.min(jnp.where(l0 == m0, lane, LANES), axis=-1, keepdims=True)
    l1 = jnp.where(lane == i0, neg, l0)
    m1 = jnp.max(l1, axis=-1, keepdims=True)
    i1 = jnp.min(jnp.where(l1 == m1, lane, LANES), axis=-1, keepdims=True)
    e = jnp.exp(m1 - m0)
    wgt0 = 1.0 / (1.0 + e)
    wgt1 = e / (1.0 + e)
    onehot = jnp.where(lane == i0, 1.0, jnp.where(lane == i1, 1.0, 0.0))
    row = lax.broadcasted_iota(jnp.int32, (tm, tm), 0)
    col = lax.broadcasted_iota(jnp.int32, (tm, tm), 1)
    tri = jnp.where(col < row, 1.0, 0.0).astype(BF16)
    before = jnp.dot(tri, onehot.astype(BF16), preferred_element_type=F32) + carry_ref[...]
    rank0 = jnp.sum(jnp.where(lane == i0, before, 0.0), axis=-1, keepdims=True)
    rank1 = jnp.sum(jnp.where(lane == i1, before, 0.0), axis=-1, keepdims=True)
    total = carry_ref[...] + jnp.sum(onehot, axis=0, keepdims=True)
    carry_ref[...] = total
    count_ref[...] = jnp.broadcast_to(total, count_ref.shape)
    route = jnp.where(lane == 0, i0.astype(F32),
                      jnp.where(lane == 1, i1.astype(F32),
                                jnp.where(lane == 2, rank0, jnp.where(lane == 3, rank1, 0.0))))
    route_ref[...] = route
    w0_ref[...] = jnp.broadcast_to(wgt0, w0_ref.shape)
    w1_ref[...] = jnp.broadcast_to(wgt1, w1_ref.shape)


def _router(h, gain, scale, shift, w_router, seq, *, tm=512):
    T, D = h.shape
    E = w_router.shape[1]
    tm = _tile(seq, tm)
    rows_per_b = seq // tm
    w_pad = jnp.zeros((D, LANES), F32).at[:, :E].set(w_router)
    kern = functools.partial(_router_kernel, n_experts=E)
    row_spec = pl.BlockSpec((tm, LANES), lambda i: (i, 0))
    row_shape = jax.ShapeDtypeStruct((T, LANES), F32)
    return pl.pallas_call(
        kern,
        grid=(T // tm,),
        in_specs=[
            pl.BlockSpec((tm, D), lambda i: (i, 0)),
            pl.BlockSpec((1, D), lambda i: (0, 0)),
            pl.BlockSpec((None, 1, D), lambda i: (i // rows_per_b, 0, 0)),
            pl.BlockSpec((None, 1, D), lambda i: (i // rows_per_b, 0, 0)),
            pl.BlockSpec((D, LANES), lambda i: (0, 0)),
        ],
        out_specs=[
            pl.BlockSpec((tm,) + _row_tile(D), lambda i: (i, 0, 0)),
            row_spec, row_spec, row_spec,
            pl.BlockSpec((8, LANES), lambda i: (0, 0)),
        ],
        out_shape=[
            jax.ShapeDtypeStruct((T,) + _row_tile(D), BF16),
            row_shape, row_shape, row_shape,
            jax.ShapeDtypeStruct((8, LANES), F32),
        ],
        scratch_shapes=[pltpu.VMEM((1, LANES), F32)],
        compiler_params=_params(("arbitrary",), 2 * tm * D * 4 + 2 * tm * D * 2 + 4 * tm * D * 4),
        name="moe_router",
    )(h, gain.reshape(1, D), scale, shift, w_pad)


def _fox_kernel(ti_ref, tj_ref, q_ref, k_ref, e_ref, v_ref, o_ref, kp_ref, vp_ref, m_ref, acc_ref,
                *bufs, tq, n_off, n_diag):
    lane = lax.broadcasted_iota(jnp.int32, k_ref.shape, 1)
    kp_ref[:FOX_DH, :] = k_ref[...].T
    kp_ref[FOX_DH:, :] = e_ref[...].T
    vp_ref[:, :FOX_DH] = v_ref[...]
    vp_ref[:, FOX_DH:] = jnp.where(lane == 0, 1.0, 0.0).astype(BF16)
    m_ref[...] = jnp.full(m_ref.shape, -jnp.inf, F32)
    acc_ref[...] = jnp.zeros_like(acc_ref)

    LAG = FOX_PIPELINE_LAG
    nbuf = 2 * LAG
    lane = lax.broadcasted_iota(jnp.int32, (tq, FOX_DH), 1)
    ones = jnp.where(lane < 3, 1.0, 0.0).astype(BF16)
    s_refs, p_refs, a_refs = bufs[:nbuf], bufs[nbuf:2 * nbuf], bufs[2 * nbuf:]

    def tile_rows(idx_ref, t):
        return pl.ds(pl.multiple_of(idx_ref[t] * tq, tq), tq)

    def logits(t, par):
        qp = jnp.concatenate([q_ref[tile_rows(ti_ref, t), :], ones], axis=-1)
        kp = kp_ref[:, tile_rows(tj_ref, t)]
        s_refs[par][...] = jnp.dot(qp, kp, preferred_element_type=F32)

    def softmax(t, par, diag):
        rows = tile_rows(ti_ref, t)
        s = s_refs[par][...]
        if diag:
            row = lax.broadcasted_iota(jnp.int32, (tq, tq), 0)
            col = lax.broadcasted_iota(jnp.int32, (tq, tq), 1)
            s = jnp.where(col <= row, s, -jnp.inf)
        m = m_ref[rows, :]
        m_new = jnp.maximum(m, jnp.max(s, axis=-1, keepdims=True))
        m_ref[rows, :] = m_new
        a_refs[par][...] = jnp.exp2(m - m_new)
        p_refs[par][...] = jnp.exp2(s - m_new).astype(BF16)

    def values(t, par, diag):
        rows = tile_rows(ti_ref, t)
        vp = vp_ref[tile_rows(tj_ref, t), :]
        acc = a_refs[par][...] * acc_ref[rows, :] + jnp.dot(p_refs[par][...], vp,
                                                            preferred_element_type=F32)
        if diag:
            o_ref[rows, :] = (acc[:, :FOX_DH] / acc[:, FOX_DH:FOX_DH + 1]).astype(o_ref.dtype)
        else:
            acc_ref[rows, :] = acc

    n = n_off + n_diag
    last = n - 1 + 2 * LAG

    def plan(u):
        tv, ts = u - 2 * LAG, u - LAG
        return (0 <= tv < n, tv >= n_off), (0 <= ts < n, ts >= n_off), u < n

    def step(u, slot, stages):
        (do_v, diag_v), (do_s, diag_s), do_l = stages
        if do_v:
            values(u - 2 * LAG, slot, diag_v)
        if do_s:
            softmax(u - LAG, (slot + LAG) % nbuf, diag_s)
        if do_l:
            logits(u, slot)

    def uniform(u0):
        plans = {plan(u0 + g) for g in range(nbuf)}
        stages = next(iter(plans))
        full = stages[0][0] and stages[1][0] and stages[2]
        return stages if len(plans) == 1 and full and u0 + nbuf - 1 <= last else None

    u = 0
    while u <= last:
        stages = uniform(u) if u % nbuf == 0 else None
        if stages is None:
            step(u, u % nbuf, plan(u))
            u += 1
            continue
        count = 1
        while uniform(u + count * nbuf) == stages:
            count += 1

        def group(d, c, base=u, stages=stages):
            for g in range(nbuf):
                step(base + nbuf * d + g, g, stages)
            return c

        lax.fori_loop(0, count, group, 0)
        u += count * nbuf


def _fox_attention(q, kv, e, batch, seq, heads, *, tq=512):
    T = q.shape[0]
    tq = _tile(seq, tq)
    nq = seq // tq
    below = [(i, j) for i in range(nq) for j in range(i)]
    tiles = below + [(i, i) for i in range(nq)]
    ti = jnp.array([t[0] for t in tiles], jnp.int32)
    tj = jnp.array([t[1] for t in tiles], jnp.int32)
    kern = functools.partial(_fox_kernel, tq=tq, n_off=len(below), n_diag=nq)
    nbuf = 2 * FOX_PIPELINE_LAG
    grid_spec = pltpu.PrefetchScalarGridSpec(
        num_scalar_prefetch=2,
        grid=(batch, heads),
        in_specs=[
            pl.BlockSpec((seq, FOX_DH), lambda b, h, ti, tj: (b, h)),
            pl.BlockSpec((seq, FOX_DH), lambda b, h, ti, tj: (b, h)),
            pl.BlockSpec((seq, FOX_DH), lambda b, h, ti, tj: (b, h)),
            pl.BlockSpec((seq, FOX_DH), lambda b, h, ti, tj: (b, heads + h)),
        ],
        out_specs=pl.BlockSpec((seq, FOX_DH), lambda b, h, ti, tj: (b, h)),
        scratch_shapes=([pltpu.VMEM((2 * FOX_DH, seq), BF16), pltpu.VMEM((seq, 2 * FOX_DH), BF16),
                         pltpu.VMEM((seq, 1), F32), pltpu.VMEM((seq, 2 * FOX_DH), F32)]
                        + [pltpu.VMEM((tq, tq), F32)] * nbuf + [pltpu.VMEM((tq, tq), BF16)] * nbuf
                        + [pltpu.VMEM((tq, 1), F32)] * nbuf),
    )
    vmem = 14 * seq * FOX_DH * 2 + seq * (LANES + 2 * FOX_DH) * 4 + nbuf * tq * (6 * tq + 4 * LANES)
    return pl.pallas_call(
        kern,
        grid_spec=grid_spec,
        out_shape=jax.ShapeDtypeStruct((T, heads * FOX_DH), BF16),
        compiler_params=_params(("arbitrary", "arbitrary"), vmem),
        name="fox_attention",
    )(ti, tj, q, kv, e, kv)


def _swiglu_step(u, w1_ref, w3_ref, w2_ref, acc_ref):
    a = jnp.dot(u, w1_ref[...].astype(BF16), preferred_element_type=F32)
    g = jnp.dot(u, w3_ref[...].astype(BF16), preferred_element_type=F32)
    mid = (a * jax.nn.sigmoid(a) * g).astype(BF16)
    acc_ref[...] += jnp.dot(mid, w2_ref[...].astype(BF16), preferred_element_type=F32)


def _dense_ffn_kernel(h_ref, gain_ref, scale_ref, shift_ref, gate_ref, w1_ref, w3_ref, w2_ref,
                      o_ref, u_ref, acc_ref):
    f = pl.program_id(1)

    @pl.when(f == 0)
    def _():
        u_ref[...] = _norm_mod(h_ref[...], gain_ref[...], scale_ref[...], shift_ref[...]).astype(BF16)
        acc_ref[...] = jnp.zeros_like(acc_ref)

    _swiglu_step(u_ref[...], w1_ref, w3_ref, w2_ref, acc_ref)

    @pl.when(f == pl.num_programs(1) - 1)
    def _():
        o_ref[...] = h_ref[...] + gate_ref[...] * acc_ref[...]


def _dense_ffn(h, gain, scale, shift, gate, w13, w2, seq, *, tm=512, tf=512):
    T, D = h.shape
    F = w2.shape[0]
    tm, tf = _tile(seq, tm), _tile(F, tf)
    nf = F // tf
    rows_per_b = seq // tm
    vec = pl.BlockSpec((None, 1, D), lambda i, f: (i // rows_per_b, 0, 0))
    vmem = 4 * tm * D * 4 + tm * D * 2 + tm * D * 4 + 2 * 3 * D * tf * 2
    return pl.pallas_call(
        _dense_ffn_kernel,
        grid=(T // tm, nf),
        in_specs=[
            pl.BlockSpec((tm, D), lambda i, f: (i, 0)),
            pl.BlockSpec((1, D), lambda i, f: (0, 0)),
            vec, vec, vec,
            pl.BlockSpec((D, tf), lambda i, f: (0, f)),
            pl.BlockSpec((D, tf), lambda i, f: (0, nf + f)),
            pl.BlockSpec((tf, D), lambda i, f: (f, 0)),
        ],
        out_specs=pl.BlockSpec((tm, D), lambda i, f: (i, 0)),
        out_shape=jax.ShapeDtypeStruct((T, D), F32),
        scratch_shapes=[pltpu.VMEM((tm, D), BF16), pltpu.VMEM((tm, D), F32)],
        compiler_params=_params(("arbitrary", "arbitrary"), vmem),
        name="dense_swiglu",
    )(h, gain.reshape(1, D), scale, shift, gate, w13, w13, w2)


def _expert_ffn_kernel(te_ref, nu_ref, x_ref, w1_ref, w3_ref, w2_ref, o_ref, x2_ref, acc_ref):
    i, f = pl.program_id(0), pl.program_id(1)
    used = i < nu_ref[0]

    @pl.when(f == 0)
    def _():
        x2_ref[...] = x_ref[...].reshape(x2_ref.shape)
        acc_ref[...] = jnp.zeros_like(acc_ref)

    @pl.when(used)
    def _():
        _swiglu_step(x2_ref[...], w1_ref, w3_ref, w2_ref, acc_ref)

    @pl.when(f == pl.num_programs(1) - 1)
    def _():
        o_ref[...] = acc_ref[...].astype(o_ref.dtype).reshape(o_ref.shape)


def _expert_ffn(xs, tile_expert, n_used, w13, w2, layer, *, tm, tf=256):
    P = xs.shape[0]
    row_tile = xs.shape[1:]
    _, E, F, D = w2.shape
    tf = _tile(F, tf)
    nf = F // tf

    def fidx(i, f, nu):
        return jnp.where(i < nu[0], f, nf - 1)

    grid_spec = pltpu.PrefetchScalarGridSpec(
        num_scalar_prefetch=2,
        grid=(P // tm, nf),
        in_specs=[
            pl.BlockSpec((tm,) + row_tile, lambda i, f, te, nu: (i, 0, 0)),
            pl.BlockSpec((None, None, D, tf), lambda i, f, te, nu: (layer, te[i], 0, fidx(i, f, nu))),
            pl.BlockSpec((None, None, D, tf), lambda i, f, te, nu: (layer, te[i], 0, nf + fidx(i, f, nu))),
            pl.BlockSpec((None, None, tf, D), lambda i, f, te, nu: (layer, te[i], fidx(i, f, nu), 0)),
        ],
        out_specs=pl.BlockSpec((tm,) + row_tile, lambda i, f, te, nu: (i, 0, 0)),
        scratch_shapes=[pltpu.VMEM((tm, D), BF16), pltpu.VMEM((tm, D), F32)],
    )
    w_bytes = jnp.dtype(w2.dtype).itemsize
    vmem = 5 * tm * D * 2 + tm * D * 4 + 3 * D * tf * (2 * w_bytes + 2) + 4 * tm * tf * 4
    return pl.pallas_call(
        _expert_ffn_kernel,
        grid_spec=grid_spec,
        out_shape=jax.ShapeDtypeStruct((P,) + row_tile, BF16),
        compiler_params=_params(("arbitrary", "arbitrary"), vmem),
        name="expert_swiglu",
    )(tile_expert, n_used, xs, w13, w13, w2)


def _dispatch_kernel(pos_ref, u_ref, xs_in_ref, xs_ref, sem, *, chunk):
    del xs_in_ref

    def copy(r):
        return pltpu.make_async_copy(u_ref.at[r // TOP_K], xs_ref.at[pos_ref[r]], sem)

    def start(r, c):
        copy(r).start()
        return c

    def wait(r, c):
        copy(r).wait()
        return c

    lax.fori_loop(0, TOP_K * chunk, start, 0, unroll=8)
    lax.fori_loop(0, TOP_K * chunk, wait, 0, unroll=8)


def _dispatch(u3, pos_flat, n_slots, *, chunk=512):
    T = u3.shape[0]
    chunk = _tile(T, chunk)
    xs0 = jnp.zeros((n_slots,) + u3.shape[1:], u3.dtype)
    kern = functools.partial(_dispatch_kernel, chunk=chunk)
    return pl.pallas_call(
        kern,
        grid=(T // chunk,),
        in_specs=[
            pl.BlockSpec((TOP_K * chunk,), lambda i: (i,), memory_space=pltpu.SMEM),
            pl.BlockSpec((chunk,) + u3.shape[1:], lambda i: (i, 0, 0)),
            pl.BlockSpec(memory_space=pl.ANY),
        ],
        out_specs=pl.BlockSpec(memory_space=pl.ANY),
        out_shape=jax.ShapeDtypeStruct(xs0.shape, xs0.dtype),
        scratch_shapes=[pltpu.SemaphoreType.DMA(())],
        input_output_aliases={2: 0},
        compiler_params=_params(("arbitrary",), 4 * chunk * u3.shape[1] * u3.shape[2] * 2),
        name="moe_dispatch",
    )(pos_flat, u3, xs0)


def _combine_kernel(pos_ref, ys_ref, w0_ref, w1_ref, h_ref, gate_ref, o_ref, buf_ref, sem, *, chunk):
    def copy(r):
        return pltpu.make_async_copy(ys_ref.at[pos_ref[r]], buf_ref.at[r % TOP_K, r // TOP_K], sem)

    def start(r, c):
        copy(r).start()
        return c

    def wait(r, c):
        copy(r).wait()
        return c

    lax.fori_loop(0, TOP_K * chunk, start, 0, unroll=8)
    lax.fori_loop(0, TOP_K * chunk, wait, 0, unroll=8)
    y = w0_ref[...] * buf_ref[0].astype(F32) + w1_ref[...] * buf_ref[1].astype(F32)
    o_ref[...] = h_ref[...] + gate_ref[...] * y.reshape(o_ref.shape)


def _combine(ys3, pos_flat, w0, w1, h, gate, seq, *, chunk=512):
    T, D = h.shape
    tile = ys3.shape[1:]
    chunk = _tile(seq, chunk)
    rows_per_b = seq // chunk
    kern = functools.partial(_combine_kernel, chunk=chunk)
    wspec = pl.BlockSpec((chunk, 1, LANES), lambda i: (i, 0, 0))
    return pl.pallas_call(
        kern,
        grid=(T // chunk,),
        in_specs=[
            pl.BlockSpec((TOP_K * chunk,), lambda i: (i,), memory_space=pltpu.SMEM),
            pl.BlockSpec(memory_space=pl.ANY),
            wspec, wspec,
            pl.BlockSpec((chunk, D), lambda i: (i, 0)),
            pl.BlockSpec((None, 1, D), lambda i: (i // rows_per_b, 0, 0)),
        ],
        out_specs=pl.BlockSpec((chunk, D), lambda i: (i, 0)),
        out_shape=jax.ShapeDtypeStruct((T, D), F32),
        scratch_shapes=[pltpu.VMEM((TOP_K, chunk) + tile, BF16), pltpu.SemaphoreType.DMA(())],
        compiler_params=_params(("arbitrary",), 10 * chunk * D * 4),
        name="moe_combine",
    )(pos_flat, ys3, w0.reshape(T, 1, LANES), w1.reshape(T, 1, LANES), h, gate)


def _moe_layer(h, gain, scale, shift, gate, w_router, w13, w2, layer, seq, *, tm=1024):
    T, D = h.shape
    E = w_router.shape[1]
    u3, route, w0, w1, counts = _router(h, gain, scale, shift, w_router, seq)

    expert = route[:, :TOP_K].astype(jnp.int32)
    rank = route[:, TOP_K:2 * TOP_K].astype(jnp.int32)
    count = counts[0, :E].astype(jnp.int32)
    padded = ((count + tm - 1) // tm) * tm
    ends = jnp.cumsum(padded)
    starts = ends - padded
    pos_flat = (starts[expert] + rank).reshape(-1)
    n_tiles = (TOP_K * T) // tm + E
    tile_start = jnp.arange(n_tiles, dtype=jnp.int32) * tm
    n_used = (ends[-1] // tm).astype(jnp.int32).reshape(1)
    tile_expert = jnp.minimum(jnp.searchsorted(ends, tile_start, side="right"), E - 1).astype(jnp.int32)
    tile_expert = jnp.where(tile_start < ends[-1], tile_expert, tile_expert[jnp.maximum(n_used[0] - 1, 0)])

    xs3 = _dispatch(u3, pos_flat, n_tiles * tm)
    ys3 = _expert_ffn(xs3, tile_expert, n_used, w13, w2, layer, tm=tm)
    return _combine(ys3, pos_flat, w0, w1, h, gate, seq)


def kernel(x, c, norm_gain, mod_w, mod_b, ret_w_in, ret_w_out, fox_w_q, fox_q_gain, fox_w_out,
           kv_norm_gain, kv_mod_w, kv_mod_b, kv_w, kv_forget_bias, kv_k_gain,
           ffn_w13, ffn_w2, moe_router, moe_w13, moe_w2):
    B, S, D = x.shape
    T = B * S
    depth = mod_w.shape[0]
    n_ret = ret_w_in.shape[0]
    ret_heads = ret_w_out.shape[1] // RET_DV
    fox_heads = fox_w_q.shape[2] // FOX_DH
    fox_width = fox_heads * FOX_DH

    c8 = jnp.zeros((8, D), F32).at[:B].set(c)
    mod = _mod_call(c8, mod_w, mod_b)[:, :B]
    kv_mod = _mod_call(c8, kv_mod_w[None], kv_mod_b[None])[0, :B]

    def vecs(m, n):
        return [v.reshape(B, 1, D) for v in jnp.split(m, n, axis=-1)]

    half = RET_DK // 2
    inv = 1.0 / (ROPE_BASE ** (jnp.arange(half, dtype=F32) / half))
    ang = jnp.arange(S, dtype=F32)[:, None] * inv[None, :]
    rope = (jnp.cos(ang), jnp.sin(ang))

    h = x.reshape(T, D)
    kv_sh = decay_sh = None
    for i in range(depth):
        sh1, sc1, g1, sh2, sc2, g2 = vecs(mod[i], 6)
        if i < n_ret:
            proj = _nm_matmul(h, norm_gain[i, 0], sc1, sh1, ret_w_in[i].astype(BF16), S, mode="rotary",
                              special_cols=2 * ret_heads * RET_DK, rope=rope, head_scale=RET_DK ** -0.5)
            y = _retention(proj, B, S, ret_heads)
            h = _proj_residual(y, ret_w_out[i].astype(BF16), h, g1, S)
        else:
            j = i - n_ret
            q = _nm_matmul(h, norm_gain[i, 0], sc1, sh1, fox_w_q[j].astype(BF16), S,
                           head_gain=fox_q_gain[j], special_cols=fox_width,
                           head_scale=LOG2E * FOX_DH ** -0.5)
            y = _fox_attention(q, kv_sh, decay_sh, B, S, fox_heads)
            h = _proj_residual(y, fox_w_out[j].astype(BF16), h, g1, S, tn=1024)
        if i % 2 == 0:
            h = _dense_ffn(h, norm_gain[i, 1], sc2, sh2, g2,
                           ffn_w13[i // 2].astype(BF16), ffn_w2[i // 2].astype(BF16), S)
        else:
            h = _moe_layer(h, norm_gain[i, 1], sc2, sh2, g2, moe_router[i // 2], moe_w13, moe_w2, i // 2, S)
        if i == n_ret - 1:
            ksh, ksc = vecs(kv_mod, 2)
            kv_sh = _nm_matmul(h, kv_norm_gain, ksc, ksh, kv_w[:, :2 * fox_width].astype(BF16), S,
                               head_gain=kv_k_gain, special_cols=fox_width)
            decay_sh = _forget_cumsum(h, kv_norm_gain, ksc, ksh, kv_w[:, 2 * fox_width:], kv_forget_bias, S)
    return h.reshape(B, S, D)
```

```python
import functools
import math

import jax
import jax.numpy as jnp
from jax import lax
from jax.experimental import pallas as pl
from jax.experimental.pallas import tpu as pltpu

F32 = jnp.float32
BF16 = jnp.bfloat16
HIGHEST = lax.Precision.HIGHEST

EPS = 1e-6
ROPE_BASE = 10000.0
RET_DK = 256
RET_DV = 2 * RET_DK
FOX_DH = 128
FOX_PIPELINE_LAG = 2
TOP_K = 2
LOG2E = math.log2(math.e)
LANES = 128
SUBLANES_BF16 = 16
VMEM_LIMIT_CAP = 56 * 1024 * 1024


def _params(semantics, vmem_bytes):
    limit = int(min(max(2 * vmem_bytes, 16 * 1024 * 1024), VMEM_LIMIT_CAP))
    return pltpu.CompilerParams(dimension_semantics=semantics, vmem_limit_bytes=limit)


def _tile(n, pref):
    t = min(n, pref)
    assert n % t == 0, (n, pref)
    return t


def _row_tile(d):
    assert d % (SUBLANES_BF16 * LANES) == 0, d
    return (d // LANES, LANES)


def _norm_mod(h, gain, scale, shift):
    ms = jnp.mean(h * h, axis=-1, keepdims=True)
    return h * lax.rsqrt(ms + EPS) * gain * (1.0 + scale) + shift


def _mod_kernel(c_ref, w_ref, b_ref, o_ref):
    c = c_ref[...]
    cond = c * jax.nn.sigmoid(c)
    o_ref[...] = jnp.dot(cond, w_ref[...], precision=HIGHEST, preferred_element_type=F32) + b_ref[...]


def _mod_call(c8, w, b):
    L, D, N = w.shape
    tn = _tile(N, 1024)
    return pl.pallas_call(
        _mod_kernel,
        grid=(L, N // tn),
        in_specs=[
            pl.BlockSpec((8, D), lambda l, j: (0, 0)),
            pl.BlockSpec((None, D, tn), lambda l, j: (l, 0, j)),
            pl.BlockSpec((None, 1, tn), lambda l, j: (l, 0, j)),
        ],
        out_specs=pl.BlockSpec((None, 8, tn), lambda l, j: (l, 0, j)),
        out_shape=jax.ShapeDtypeStruct((L, 8, N), F32),
        compiler_params=_params(("arbitrary", "arbitrary"), 2 * D * tn * 4),
        name="adaln_mod",
    )(c8, w, b.reshape(L, 1, N))


def _nm_matmul_kernel(h_ref, gain_ref, scale_ref, shift_ref, w_ref, aux_a_ref, aux_b_ref, o_ref, u_ref,
                      *, mode, n_special, head_scale):
    j = pl.program_id(1)

    @pl.when(j == 0)
    def _():
        u_ref[...] = _norm_mod(h_ref[...], gain_ref[...], scale_ref[...], shift_ref[...]).astype(BF16)

    acc = jnp.dot(u_ref[...], w_ref[...], preferred_element_type=F32)

    if n_special == 0:
        o_ref[...] = acc.astype(o_ref.dtype)
        return

    @pl.when(j < n_special)
    def _():
        if mode == "head_norm":
            hg = aux_a_ref[...] * head_scale
            for c in range(acc.shape[1] // FOX_DH):
                blk = acc[:, c * FOX_DH:(c + 1) * FOX_DH]
                ms = jnp.mean(blk * blk, axis=-1, keepdims=True)
                o_ref[:, c * FOX_DH:(c + 1) * FOX_DH] = (blk * lax.rsqrt(ms + EPS) * hg).astype(o_ref.dtype)
        else:
            half = RET_DK // 2
            k_scale = jnp.where(j >= n_special // 2, head_scale, 1.0)
            cos, sin = aux_a_ref[...] * k_scale, aux_b_ref[...] * k_scale
            for c in range(acc.shape[1] // RET_DK):
                x1 = acc[:, c * RET_DK:c * RET_DK + half]
                x2 = acc[:, c * RET_DK + half:(c + 1) * RET_DK]
                o_ref[:, c * RET_DK:c * RET_DK + half] = (x1 * cos - x2 * sin).astype(o_ref.dtype)
                o_ref[:, c * RET_DK + half:(c + 1) * RET_DK] = (x1 * sin + x2 * cos).astype(o_ref.dtype)

    @pl.when(j >= n_special)
    def _():
        o_ref[...] = acc.astype(o_ref.dtype)


def _nm_matmul(h, gain, scale, shift, w, seq, *, mode="head_norm", special_cols=0, head_gain=None,
               rope=None, head_scale=1.0, tm=1024, tn=1024):
    T, D = h.shape
    N = w.shape[1]
    tm, tn = _tile(seq, tm), _tile(N, tn)
    assert special_cols % tn == 0
    rows_per_b = seq // tm
    if mode == "rotary":
        aux = rope
        aux_spec = pl.BlockSpec((tm, RET_DK // 2), lambda i, j: (i % rows_per_b, 0))
    else:
        gain_row = (jnp.ones((FOX_DH,), F32) if head_gain is None else head_gain).reshape(1, FOX_DH)
        aux = (gain_row, gain_row)
        aux_spec = pl.BlockSpec((1, FOX_DH), lambda i, j: (0, 0))
    vmem = 2 * tm * D * 4 + 2 * D * tn * 2 + 2 * tm * tn * 2 + tm * D * 2 + 4 * tm * LANES * 4
    kern = functools.partial(_nm_matmul_kernel, mode=mode, n_special=special_cols // tn,
                             head_scale=head_scale)
    return pl.pallas_call(
        kern,
        grid=(T // tm, N // tn),
        in_specs=[
            pl.BlockSpec((tm, D), lambda i, j: (i, 0)),
            pl.BlockSpec((1, D), lambda i, j: (0, 0)),
            pl.BlockSpec((None, 1, D), lambda i, j: (i // rows_per_b, 0, 0)),
            pl.BlockSpec((None, 1, D), lambda i, j: (i // rows_per_b, 0, 0)),
            pl.BlockSpec((D, tn), lambda i, j: (0, j)),
            aux_spec, aux_spec,
        ],
        out_specs=pl.BlockSpec((tm, tn), lambda i, j: (i, j)),
        out_shape=jax.ShapeDtypeStruct((T, N), BF16),
        scratch_shapes=[pltpu.VMEM((tm, D), BF16)],
        compiler_params=_params(("arbitrary", "arbitrary"), vmem),
        name="norm_mod_proj",
    )(h, gain.reshape(1, D), scale, shift, w, *aux)


def _proj_residual_kernel(y_ref, w_ref, h_ref, gate_ref, o_ref):
    acc = jnp.dot(y_ref[...], w_ref[...], preferred_element_type=F32)
    o_ref[...] = h_ref[...] + gate_ref[...] * acc


def _proj_residual(y, w, h, gate, seq, *, tm=1024, tn=512):
    T, K = y.shape
    D = w.shape[1]
    tm, tn = _tile(seq, tm), _tile(D, tn)
    rows_per_b = seq // tm
    vmem = 2 * tm * K * 2 + 2 * K * tn * 2 + 4 * tm * tn * 4
    return pl.pallas_call(
        _proj_residual_kernel,
        grid=(T // tm, D // tn),
        in_specs=[
            pl.BlockSpec((tm, K), lambda i, j: (i, 0)),
            pl.BlockSpec((K, tn), lambda i, j: (0, j)),
            pl.BlockSpec((tm, tn), lambda i, j: (i, j)),
            pl.BlockSpec((None, 1, tn), lambda i, j: (i // rows_per_b, 0, j)),
        ],
        out_specs=pl.BlockSpec((tm, tn), lambda i, j: (i, j)),
        out_shape=jax.ShapeDtypeStruct((T, D), F32),
        compiler_params=_params(("arbitrary", "arbitrary"), vmem),
        name="proj_residual",
    )(y, w, h, gate)


def _retention_kernel(lg_ref, q_ref, k_ref, v_ref, g_ref, o_ref, state_ref, *, chunk, n_chunks):
    @pl.when(pl.program_id(2) == 0)
    def _():
        state_ref[...] = jnp.zeros_like(state_ref)

    lg = lg_ref[pl.program_id(1)]
    row = lax.broadcasted_iota(jnp.int32, (chunk, chunk), 0)
    col = lax.broadcasted_iota(jnp.int32, (chunk, chunk), 1)
    dist = (row - col).astype(F32)
    d_local = jnp.where(dist >= 0, jnp.exp(lg * jnp.maximum(dist, 0.0)), 0.0)
    t = lax.broadcasted_iota(jnp.int32, (chunk, 1), 0).astype(F32)
    xi = jnp.exp(lg * (t + 1.0))
    zeta = jnp.exp(lg * (chunk - 1.0 - t))
    gamma_c = jnp.exp(jnp.full((1, 1), lg * chunk, F32))

    def body(ci, carry):
        r0 = pl.multiple_of(ci * chunk, chunk)
        q = q_ref[pl.ds(r0, chunk), :]
        k = k_ref[pl.ds(r0, chunk), :]
        v = v_ref[pl.ds(r0, chunk), :]
        scores = lax.dot_general(q, k, (((1,), (1,)), ((), ())), preferred_element_type=F32) * d_local
        inner = jnp.dot(scores.astype(BF16), v, preferred_element_type=F32)
        state = state_ref[...]
        cross = jnp.dot(q, state.astype(BF16), preferred_element_type=F32) * xi
        vz = (v.astype(F32) * zeta).astype(BF16)
        state_ref[...] = gamma_c * state + lax.dot_general(
            k, vz, (((0,), (0,)), ((), ())), preferred_element_type=F32)
        o = inner + cross
        ms = jnp.mean(o * o, axis=-1, keepdims=True)
        g = g_ref[pl.ds(r0, chunk), :].astype(F32)
        o_ref[pl.ds(r0, chunk), :] = (g * jax.nn.sigmoid(g) * (o * lax.rsqrt(ms + EPS))).astype(o_ref.dtype)
        return carry

    lax.fori_loop(0, n_chunks, body, 0, unroll=True)


def _retention(proj, batch, seq, heads, *, chunk=256, rows=1024):
    T = proj.shape[0]
    chunk = _tile(seq, chunk)
    rows = _tile(seq, rows)
    nblk = seq // rows
    log_gamma = jnp.log1p(-jnp.exp2(-5.0 - jnp.arange(heads, dtype=F32)))
    kern = functools.partial(_retention_kernel, chunk=chunk, n_chunks=rows // chunk)
    qk_blocks = heads
    v_off = 2 * heads * RET_DK // RET_DV
    return pl.pallas_call(
        kern,
        grid=(batch, heads, nblk),
        in_specs=[
            pl.BlockSpec(memory_space=pltpu.SMEM),
            pl.BlockSpec((rows, RET_DK), lambda b, h, c: (b * nblk + c, h)),
            pl.BlockSpec((rows, RET_DK), lambda b, h, c: (b * nblk + c, qk_blocks + h)),
            pl.BlockSpec((rows, RET_DV), lambda b, h, c: (b * nblk + c, v_off + h)),
            pl.BlockSpec((rows, RET_DV), lambda b, h, c: (b * nblk + c, v_off + heads + h)),
        ],
        out_specs=pl.BlockSpec((rows, RET_DV), lambda b, h, c: (b * nblk + c, h)),
        out_shape=jax.ShapeDtypeStruct((T, heads * RET_DV), BF16),
        scratch_shapes=[pltpu.VMEM((RET_DK, RET_DV), F32)],
        compiler_params=_params(("arbitrary", "arbitrary", "arbitrary"), 16 * 1024 * 1024),
        name="retention",
    )(log_gamma, proj, proj, proj, proj)


def _bf16_pieces(x, n):
    pieces = []
    for _ in range(n):
        p = x.astype(BF16)
        pieces.append(p)
        x = x - p.astype(F32)
    return pieces


def _small_proj(h_ref, gain_ref, scale_ref, shift_ref, w_ref):
    u = _norm_mod(h_ref[...], gain_ref[...], scale_ref[...], shift_ref[...])
    u_hi, u_lo = _bf16_pieces(u, 2)
    w_hi, w_lo = _bf16_pieces(w_ref[...], 2)
    out = (jnp.dot(u_hi, w_hi, preferred_element_type=F32) + jnp.dot(u_hi, w_lo, preferred_element_type=F32)
           + jnp.dot(u_lo, w_hi, preferred_element_type=F32))
    return u, out


def _forget_kernel(h_ref, gain_ref, scale_ref, shift_ref, w_ref, bias_ref, o_ref, carry_ref,
                   *, tiles_per_seq, n_heads):
    @pl.when(pl.program_id(0) % tiles_per_seq == 0)
    def _():
        carry_ref[...] = jnp.zeros_like(carry_ref)

    _, f = _small_proj(h_ref, gain_ref, scale_ref, shift_ref, w_ref)
    log_f = jax.nn.log_sigmoid(f + bias_ref[...])
    tm = log_f.shape[0]
    row = lax.broadcasted_iota(jnp.int32, (tm, tm), 0)
    col = lax.broadcasted_iota(jnp.int32, (tm, tm), 1)
    tri = jnp.where(col <= row, 1.0, 0.0).astype(BF16)
    cum = sum(jnp.dot(tri, piece, preferred_element_type=F32) for piece in _bf16_pieces(log_f, 3))
    cum = cum + carry_ref[...]
    carry_ref[...] = cum[tm - 1:tm, :]
    neg = cum * (-LOG2E)
    lane = lax.broadcasted_iota(jnp.int32, (tm, FOX_DH), 1)
    for hh in range(n_heads):
        b = neg[:, hh:hh + 1]
        hi = b.astype(BF16).astype(F32)
        mid = (b - hi).astype(BF16).astype(F32)
        lo = b - hi - mid
        blk = jnp.where(lane == 0, hi, jnp.where(lane == 1, mid, jnp.where(lane == 2, lo, 0.0)))
        o_ref[:, hh * FOX_DH:(hh + 1) * FOX_DH] = blk.astype(o_ref.dtype)


def _forget_cumsum(h, gain, scale, shift, w_f, bias, seq, *, tm=512):
    T, D = h.shape
    nh = w_f.shape[1]
    tm = _tile(seq, tm)
    rows_per_b = seq // tm
    w_pad = jnp.zeros((D, LANES), F32).at[:, :nh].set(w_f)
    b_pad = jnp.zeros((1, LANES), F32).at[0, :nh].set(bias)
    kern = functools.partial(_forget_kernel, tiles_per_seq=rows_per_b, n_heads=nh)
    return pl.pallas_call(
        kern,
        grid=(T // tm,),
        in_specs=[
            pl.BlockSpec((tm, D), lambda i: (i, 0)),
            pl.BlockSpec((1, D), lambda i: (0, 0)),
            pl.BlockSpec((None, 1, D), lambda i: (i // rows_per_b, 0, 0)),
            pl.BlockSpec((None, 1, D), lambda i: (i // rows_per_b, 0, 0)),
            pl.BlockSpec((D, LANES), lambda i: (0, 0)),
            pl.BlockSpec((1, LANES), lambda i: (0, 0)),
        ],
        out_specs=pl.BlockSpec((tm, nh * FOX_DH), lambda i: (i, 0)),
        out_shape=jax.ShapeDtypeStruct((T, nh * FOX_DH), BF16),
        scratch_shapes=[pltpu.VMEM((1, LANES), F32)],
        compiler_params=_params(("arbitrary",), 2 * tm * D * 4 + 4 * tm * D * 4),
        name="forget_cumsum",
    )(h, gain.reshape(1, D), scale, shift, w_pad, b_pad)


def _router_kernel(h_ref, gain_ref, scale_ref, shift_ref, w_ref, u_ref, route_ref, w0_ref, w1_ref,
                   count_ref, carry_ref, *, n_experts):
    @pl.when(pl.program_id(0) == 0)
    def _():
        carry_ref[...] = jnp.zeros_like(carry_ref)

    u, logits = _small_proj(h_ref, gain_ref, scale_ref, shift_ref, w_ref)
    u_ref[...] = u.reshape(u_ref.shape).astype(u_ref.dtype)
    tm = logits.shape[0]
    lane = lax.broadcasted_iota(jnp.int32, (tm, LANES), 1)
    neg = -jnp.inf
    l0 = jnp.where(lane < n_experts, logits, neg)
    m0 = jnp.max(l0, axis=-1, keepdims=True)
    i0 = jnp.min(jnp.where(l0 == m0, lane, LANES), axis=-1, keepdims=True)
    l1 = jnp.where(lane == i0, neg, l0)
    m1 = jnp.max(l1, axis=-1, keepdims=True)
    i1 = jnp.min(jnp.where(l1 == m1, lane, LANES), axis=-1, keepdims=True)
    e = jnp.exp(m1 - m0)
    wgt0 = 1.0 / (1.0 + e)
    wgt1 = e / (1.0 + e)
    onehot = jnp.where(lane == i0, 1.0, jnp.where(lane == i1, 1.0, 0.0))
    row = lax.broadcasted_iota(jnp.int32, (tm, tm), 0)
    col = lax.broadcasted_iota(jnp.int32, (tm, tm), 1)
    tri = jnp.where(col < row, 1.0, 0.0).astype(BF16)
    before = jnp.dot(tri, onehot.astype(BF16), preferred_element_type=F32) + carry_ref[...]
    rank0 = jnp.sum(jnp.where(lane == i0, before, 0.0), axis=-1, keepdims=True)
    rank1 = jnp.sum(jnp.where(lane == i1, before, 0.0), axis=-1, keepdims=True)
    total = carry_ref[...] + jnp.sum(onehot, axis=0, keepdims=True)
    carry_ref[...] = total
    count_ref[...] = jnp.broadcast_to(total, count_ref.shape)
    route = jnp.where(lane == 0, i0.astype(F32),
                      jnp.where(lane == 1, i1.astype(F32),
                                jnp.where(lane == 2, rank0, jnp.where(lane == 3, rank1, 0.0))))
    route_ref[...] = route
    w0_ref[...] = jnp.broadcast_to(wgt0, w0_ref.shape)
    w1_ref[...] = jnp.broadcast_to(wgt1, w1_ref.shape)


def _router(h, gain, scale, shift, w_router, seq, *, tm=512):
    T, D = h.shape
    E = w_router.shape[1]
    tm = _tile(seq, tm)
    rows_per_b = seq // tm
    w_pad = jnp.zeros((D, LANES), F32).at[:, :E].set(w_router)
    kern = functools.partial(_router_kernel, n_experts=E)
    row_spec = pl.BlockSpec((tm, LANES), lambda i: (i, 0))
    row_shape = jax.ShapeDtypeStruct((T, LANES), F32)
    return pl.pallas_call(
        kern,
        grid=(T // tm,),
        in_specs=[
            pl.BlockSpec((tm, D), lambda i: (i, 0)),
            pl.BlockSpec((1, D), lambda i: (0, 0)),
            pl.BlockSpec((None, 1, D), lambda i: (i // rows_per_b, 0, 0)),
            pl.BlockSpec((None, 1, D), lambda i: (i // rows_per_b, 0, 0)),
            pl.BlockSpec((D, LANES), lambda i: (0, 0)),
        ],
        out_specs=[
            pl.BlockSpec((tm,) + _row_tile(D), lambda i: (i, 0, 0)),
            row_spec, row_spec, row_spec,
            pl.BlockSpec((8, LANES), lambda i: (0, 0)),
        ],
        out_shape=[
            jax.ShapeDtypeStruct((T,) + _row_tile(D), BF16),
            row_shape, row_shape, row_shape,
            jax.ShapeDtypeStruct((8, LANES), F32),
        ],
        scratch_shapes=[pltpu.VMEM((1, LANES), F32)],
        compiler_params=_params(("arbitrary",), 2 * tm * D * 4 + 2 * tm * D * 2 + 4 * tm * D * 4),
        name="moe_router",
    )(h, gain.reshape(1, D), scale, shift, w_pad)


def _fox_kernel(ti_ref, tj_ref, q_ref, k_ref, e_ref, v_ref, o_ref, kp_ref, vp_ref, m_ref, acc_ref,
                *bufs, tq, n_off, n_diag):
    lane = lax.broadcasted_iota(jnp.int32, k_ref.shape, 1)
    kp_ref[:FOX_DH, :] = k_ref[...].T
    kp_ref[FOX_DH:, :] = e_ref[...].T
    vp_ref[:, :FOX_DH] = v_ref[...]
    vp_ref[:, FOX_DH:] = jnp.where(lane == 0, 1.0, 0.0).astype(BF16)
    m_ref[...] = jnp.full(m_ref.shape, -jnp.inf, F32)
    acc_ref[...] = jnp.zeros_like(acc_ref)

    LAG = FOX_PIPELINE_LAG
    nbuf = 2 * LAG
    lane = lax.broadcasted_iota(jnp.int32, (tq, FOX_DH), 1)
    ones = jnp.where(lane < 3, 1.0, 0.0).astype(BF16)
    s_refs, p_refs, a_refs = bufs[:nbuf], bufs[nbuf:2 * nbuf], bufs[2 * nbuf:]

    def tile_rows(idx_ref, t):
        return pl.ds(pl.multiple_of(idx_ref[t] * tq, tq), tq)

    def logits(t, par):
        qp = jnp.concatenate([q_ref[tile_rows(ti_ref, t), :], ones], axis=-1)
        kp = kp_ref[:, tile_rows(tj_ref, t)]
        s_refs[par][...] = jnp.dot(qp, kp, preferred_element_type=F32)

    def softmax(t, par, diag):
        rows = tile_rows(ti_ref, t)
        s = s_refs[par][...]
        if diag:
            row = lax.broadcasted_iota(jnp.int32, (tq, tq), 0)
            col = lax.broadcasted_iota(jnp.int32, (tq, tq), 1)
            s = jnp.where(col <= row, s, -jnp.inf)
        m = m_ref[rows, :]
        m_new = jnp.maximum(m, jnp.max(s, axis=-1, keepdims=True))
        m_ref[rows, :] = m_new
        a_refs[par][...] = jnp.exp2(m - m_new)
        p_refs[par][...] = jnp.exp2(s - m_new).astype(BF16)

    def values(t, par, diag):
        rows = tile_rows(ti_ref, t)
        vp = vp_ref[tile_rows(tj_ref, t), :]
        acc = a_refs[par][...] * acc_ref[rows, :] + jnp.dot(p_refs[par][...], vp,
                                                            preferred_element_type=F32)
        if diag:
            o_ref[rows, :] = (acc[:, :FOX_DH] / acc[:, FOX_DH:FOX_DH + 1]).astype(o_ref.dtype)
        else:
            acc_ref[rows, :] = acc

    n = n_off + n_diag
    last = n - 1 + 2 * LAG

    def plan(u):
        tv, ts = u - 2 * LAG, u - LAG
        return (0 <= tv < n, tv >= n_off), (0 <= ts < n, ts >= n_off), u < n

    def step(u, slot, stages):
        (do_v, diag_v), (do_s, diag_s), do_l = stages
        if do_v:
            values(u - 2 * LAG, slot, diag_v)
        if do_s:
            softmax(u - LAG, (slot + LAG) % nbuf, diag_s)
        if do_l:
            logits(u, slot)

    def uniform(u0):
        plans = {plan(u0 + g) for g in range(nbuf)}
        stages = next(iter(plans))
        full = stages[0][0] and stages[1][0] and stages[2]
        return stages if len(plans) == 1 and full and u0 + nbuf - 1 <= last else None

    u = 0
    while u <= last:
        stages = uniform(u) if u % nbuf == 0 else None
        if stages is None:
            step(u, u % nbuf, plan(u))
            u += 1
            continue
        count = 1
        while uniform(u + count * nbuf) == stages:
            count += 1

        def group(d, c, base=u, stages=stages):
            for g in range(nbuf):
                step(base + nbuf * d + g, g, stages)
            return c

        lax.fori_loop(0, count, group, 0)
        u += count * nbuf


def _fox_attention(q, kv, e, batch, seq, heads, *, tq=512):
    T = q.shape[0]
    tq = _tile(seq, tq)
    nq = seq // tq
    below = [(i, j) for i in range(nq) for j in range(i)]
    tiles = below + [(i, i) for i in range(nq)]
    ti = jnp.array([t[0] for t in tiles], jnp.int32)
    tj = jnp.array([t[1] for t in tiles], jnp.int32)
    kern = functools.partial(_fox_kernel, tq=tq, n_off=len(below), n_diag=nq)
    nbuf = 2 * FOX_PIPELINE_LAG
    grid_spec = pltpu.PrefetchScalarGridSpec(
        num_scalar_prefetch=2,
        grid=(batch, heads),
        in_specs=[
            pl.BlockSpec((seq, FOX_DH), lambda b, h, ti, tj: (b, h)),
            pl.BlockSpec((seq, FOX_DH), lambda b, h, ti, tj: (b, h)),
            pl.BlockSpec((seq, FOX_DH), lambda b, h, ti, tj: (b, h)),
            pl.BlockSpec((seq, FOX_DH), lambda b, h, ti, tj: (b, heads + h)),
        ],
        out_specs=pl.BlockSpec((seq, FOX_DH), lambda b, h, ti, tj: (b, h)),
        scratch_shapes=([pltpu.VMEM((2 * FOX_DH, seq), BF16), pltpu.VMEM((seq, 2 * FOX_DH), BF16),
                         pltpu.VMEM((seq, 1), F32), pltpu.VMEM((seq, 2 * FOX_DH), F32)]
                        + [pltpu.VMEM((tq, tq), F32)] * nbuf + [pltpu.VMEM((tq, tq), BF16)] * nbuf
                        + [pltpu.VMEM((tq, 1), F32)] * nbuf),
    )
    vmem = 14 * seq * FOX_DH * 2 + seq * (LANES + 2 * FOX_DH) * 4 + nbuf * tq * (6 * tq + 4 * LANES)
    return pl.pallas_call(
        kern,
        grid_spec=grid_spec,
        out_shape=jax.ShapeDtypeStruct((T, heads * FOX_DH), BF16),
        compiler_params=_params(("arbitrary", "arbitrary"), vmem),
        name="fox_attention",
    )(ti, tj, q, kv, e, kv)


def _swiglu_step(u, w1_ref, w3_ref, w2_ref, acc_ref):
    a = jnp.dot(u, w1_ref[...].astype(BF16), preferred_element_type=F32)
    g = jnp.dot(u, w3_ref[...].astype(BF16), preferred_element_type=F32)
    mid = (a * jax.nn.sigmoid(a) * g).astype(BF16)
    acc_ref[...] += jnp.dot(mid, w2_ref[...].astype(BF16), preferred_element_type=F32)


def _dense_ffn_kernel(h_ref, gain_ref, scale_ref, shift_ref, gate_ref, w1_ref, w3_ref, w2_ref,
                      o_ref, u_ref, acc_ref):
    f = pl.program_id(1)

    @pl.when(f == 0)
    def _():
        u_ref[...] = _norm_mod(h_ref[...], gain_ref[...], scale_ref[...], shift_ref[...]).astype(BF16)
        acc_ref[...] = jnp.zeros_like(acc_ref)

    _swiglu_step(u_ref[...], w1_ref, w3_ref, w2_ref, acc_ref)

    @pl.when(f == pl.num_programs(1) - 1)
    def _():
        o_ref[...] = h_ref[...] + gate_ref[...] * acc_ref[...]


def _dense_ffn(h, gain, scale, shift, gate, w13, w2, seq, *, tm=512, tf=512):
    T, D = h.shape
    F = w2.shape[0]
    tm, tf = _tile(seq, tm), _tile(F, tf)
    nf = F // tf
    rows_per_b = seq // tm
    vec = pl.BlockSpec((None, 1, D), lambda i, f: (i // rows_per_b, 0, 0))
    vmem = 4 * tm * D * 4 + tm * D * 2 + tm * D * 4 + 2 * 3 * D * tf * 2
    return pl.pallas_call(
        _dense_ffn_kernel,
        grid=(T // tm, nf),
        in_specs=[
            pl.BlockSpec((tm, D), lambda i, f: (i, 0)),
            pl.BlockSpec((1, D), lambda i, f: (0, 0)),
            vec, vec, vec,
            pl.BlockSpec((D, tf), lambda i, f: (0, f)),
            pl.BlockSpec((D, tf), lambda i, f: (0, nf + f)),
            pl.BlockSpec((tf, D), lambda i, f: (f, 0)),
        ],
        out_specs=pl.BlockSpec((tm, D), lambda i, f: (i, 0)),
        out_shape=jax.ShapeDtypeStruct((T, D), F32),
        scratch_shapes=[pltpu.VMEM((tm, D), BF16), pltpu.VMEM((tm, D), F32)],
        compiler_params=_params(("arbitrary", "arbitrary"), vmem),
        name="dense_swiglu",
    )(h, gain.reshape(1, D), scale, shift, gate, w13, w13, w2)


def _expert_ffn_kernel(te_ref, tr_ref, x_ref, w1_ref, w3_ref, w2_ref, o_ref, x2_ref, acc_ref):
    i, f = pl.program_id(0), pl.program_id(1)
    rows = tr_ref[i]
    quarter = x2_ref.shape[0] // 4

    @pl.when(f == 0)
    def _():
        x2_ref[...] = x_ref[...].reshape(x2_ref.shape)
        acc_ref[...] = jnp.zeros_like(acc_ref)

    for q in range(1, 5):
        @pl.when(jnp.logical_and(rows > (q - 1) * quarter, rows <= q * quarter))
        def _(n=q * quarter):
            _swiglu_step(x2_ref[:n, :], w1_ref, w3_ref, w2_ref, acc_ref.at[pl.ds(0, n), :])

    @pl.when(f == pl.num_programs(1) - 1)
    def _():
        o_ref[...] = acc_ref[...].astype(o_ref.dtype).reshape(o_ref.shape)


def _expert_ffn(xs, tile_expert, tile_rows, w13, w2, layer, *, tm, tf=256):
    P = xs.shape[0]
    row_tile = xs.shape[1:]
    _, E, F, D = w2.shape
    tf = _tile(F, tf)
    nf = F // tf

    def fidx(i, f, nu):
        return jnp.where(nu[i] > 0, f, nf - 1)

    grid_spec = pltpu.PrefetchScalarGridSpec(
        num_scalar_prefetch=2,
        grid=(P // tm, nf),
        in_specs=[
            pl.BlockSpec((tm,) + row_tile, lambda i, f, te, nu: (i, 0, 0)),
            pl.BlockSpec((None, None, D, tf), lambda i, f, te, nu: (layer, te[i], 0, fidx(i, f, nu))),
            pl.BlockSpec((None, None, D, tf), lambda i, f, te, nu: (layer, te[i], 0, nf + fidx(i, f, nu))),
            pl.BlockSpec((None, None, tf, D), lambda i, f, te, nu: (layer, te[i], fidx(i, f, nu), 0)),
        ],
        out_specs=pl.BlockSpec((tm,) + row_tile, lambda i, f, te, nu: (i, 0, 0)),
        scratch_shapes=[pltpu.VMEM((tm, D), BF16), pltpu.VMEM((tm, D), F32)],
    )
    w_bytes = jnp.dtype(w2.dtype).itemsize
    vmem = 5 * tm * D * 2 + tm * D * 4 + 3 * D * tf * (2 * w_bytes + 2) + 4 * tm * tf * 4
    return pl.pallas_call(
        _expert_ffn_kernel,
        grid_spec=grid_spec,
        out_shape=jax.ShapeDtypeStruct((P,) + row_tile, BF16),
        compiler_params=_params(("arbitrary", "arbitrary"), vmem),
        name="expert_swiglu",
    )(tile_expert, tile_rows, xs, w13, w13, w2)


def _dispatch_kernel(pos_ref, u_ref, xs_in_ref, xs_ref, sem, *, chunk):
    del xs_in_ref

    def copy(t, k):
        return pltpu.make_async_copy(u_ref.at[t], xs_ref.at[pos_ref[TOP_K * t + k]], sem)

    def start(t, c):
        for k in range(TOP_K):
            copy(t, k).start()
        return c

    def wait(t, c):
        for k in range(TOP_K):
            copy(t, k).wait()
        return c

    lax.fori_loop(0, chunk, start, 0, unroll=4)
    lax.fori_loop(0, chunk, wait, 0, unroll=4)


def _dispatch(u3, pos_flat, n_slots, *, chunk=512):
    T = u3.shape[0]
    chunk = _tile(T, chunk)
    xs0 = jnp.zeros((n_slots,) + u3.shape[1:], u3.dtype)
    kern = functools.partial(_dispatch_kernel, chunk=chunk)
    return pl.pallas_call(
        kern,
        grid=(T // chunk,),
        in_specs=[
            pl.BlockSpec((TOP_K * chunk,), lambda i: (i,), memory_space=pltpu.SMEM),
            pl.BlockSpec((chunk,) + u3.shape[1:], lambda i: (i, 0, 0)),
            pl.BlockSpec(memory_space=pl.ANY),
        ],
        out_specs=pl.BlockSpec(memory_space=pl.ANY),
        out_shape=jax.ShapeDtypeStruct(xs0.shape, xs0.dtype),
        scratch_shapes=[pltpu.SemaphoreType.DMA(())],
        input_output_aliases={2: 0},
        compiler_params=_params(("arbitrary",), 4 * chunk * u3.shape[1] * u3.shape[2] * 2),
        name="moe_dispatch",
    )(pos_flat, u3, xs0)


def _combine_kernel(pos_ref, ys_ref, w0_ref, w1_ref, h_ref, gate_ref, o_ref, buf_ref, sem, *, chunk):
    def copy(t, k):
        return pltpu.make_async_copy(ys_ref.at[pos_ref[TOP_K * t + k]], buf_ref.at[k, t], sem)

    def start(t, c):
        for k in range(TOP_K):
            copy(t, k).start()
        return c

    def wait(t, c):
        for k in range(TOP_K):
            copy(t, k).wait()
        return c

    lax.fori_loop(0, chunk, start, 0, unroll=4)
    lax.fori_loop(0, chunk, wait, 0, unroll=4)
    y = w0_ref[...] * buf_ref[0].astype(F32) + w1_ref[...] * buf_ref[1].astype(F32)
    o_ref[...] = h_ref[...] + gate_ref[...] * y.reshape(o_ref.shape)


def _combine(ys3, pos_flat, w0, w1, h, gate, seq, *, chunk=512):
    T, D = h.shape
    tile = ys3.shape[1:]
    chunk = _tile(seq, chunk)
    rows_per_b = seq // chunk
    kern = functools.partial(_combine_kernel, chunk=chunk)
    wspec = pl.BlockSpec((chunk, 1, LANES), lambda i: (i, 0, 0))
    return pl.pallas_call(
        kern,
        grid=(T // chunk,),
        in_specs=[
            pl.BlockSpec((TOP_K * chunk,), lambda i: (i,), memory_space=pltpu.SMEM),
            pl.BlockSpec(memory_space=pl.ANY),
            wspec, wspec,
            pl.BlockSpec((chunk, D), lambda i: (i, 0)),
            pl.BlockSpec((None, 1, D), lambda i: (i // rows_per_b, 0, 0)),
        ],
        out_specs=pl.BlockSpec((chunk, D), lambda i: (i, 0)),
        out_shape=jax.ShapeDtypeStruct((T, D), F32),
        scratch_shapes=[pltpu.VMEM((TOP_K, chunk) + tile, BF16), pltpu.SemaphoreType.DMA(())],
        compiler_params=_params(("arbitrary",), 10 * chunk * D * 4),
        name="moe_combine",
    )(pos_flat, ys3, w0.reshape(T, 1, LANES), w1.reshape(T, 1, LANES), h, gate)


def _moe_layer(h, gain, scale, shift, gate, w_router, w13, w2, layer, seq, *, tm=1024):
    T, D = h.shape
    E = w_router.shape[1]
    u3, route, w0, w1, counts = _router(h, gain, scale, shift, w_router, seq)

    expert = route[:, :TOP_K].astype(jnp.int32)
    rank = route[:, TOP_K:2 * TOP_K].astype(jnp.int32)
    count = counts[0, :E].astype(jnp.int32)
    padded = ((count + tm - 1) // tm) * tm
    ends = jnp.cumsum(padded)
    starts = ends - padded
    pos_flat = (starts[expert] + rank).reshape(-1)
    n_tiles = (TOP_K * T) // tm + E
    tile_start = jnp.arange(n_tiles, dtype=jnp.int32) * tm
    last_used = jnp.maximum(ends[-1] // tm - 1, 0)
    tile_expert = jnp.minimum(jnp.searchsorted(ends, tile_start, side="right"), E - 1).astype(jnp.int32)
    tile_rows = jnp.clip(starts[tile_expert] + count[tile_expert] - tile_start, 0, tm).astype(jnp.int32)
    tile_rows = jnp.where(tile_start < ends[-1], tile_rows, 0)
    tile_expert = jnp.where(tile_start < ends[-1], tile_expert, tile_expert[last_used])

    xs3 = _dispatch(u3, pos_flat, n_tiles * tm)
    ys3 = _expert_ffn(xs3, tile_expert, tile_rows, w13, w2, layer, tm=tm)
    return _combine(ys3, pos_flat, w0, w1, h, gate, seq)


def kernel(x, c, norm_gain, mod_w, mod_b, ret_w_in, ret_w_out, fox_w_q, fox_q_gain, fox_w_out,
           kv_norm_gain, kv_mod_w, kv_mod_b, kv_w, kv_forget_bias, kv_k_gain,
           ffn_w13, ffn_w2, moe_router, moe_w13, moe_w2):
    B, S, D = x.shape
    T = B * S
    depth = mod_w.shape[0]
    n_ret = ret_w_in.shape[0]
    ret_heads = ret_w_out.shape[1] // RET_DV
    fox_heads = fox_w_q.shape[2] // FOX_DH
    fox_width = fox_heads * FOX_DH

    c8 = jnp.zeros((8, D), F32).at[:B].set(c)
    mod = _mod_call(c8, mod_w, mod_b)[:, :B]
    kv_mod = _mod_call(c8, kv_mod_w[None], kv_mod_b[None])[0, :B]

    def vecs(m, n):
        return [v.reshape(B, 1, D) for v in jnp.split(m, n, axis=-1)]

    half = RET_DK // 2
    inv = 1.0 / (ROPE_BASE ** (jnp.arange(half, dtype=F32) / half))
    ang = jnp.arange(S, dtype=F32)[:, None] * inv[None, :]
    rope = (jnp.cos(ang), jnp.sin(ang))

    h = x.reshape(T, D)
    kv_sh = decay_sh = None
    for i in range(depth):
        sh1, sc1, g1, sh2, sc2, g2 = vecs(mod[i], 6)
        if i < n_ret:
            proj = _nm_matmul(h, norm_gain[i, 0], sc1, sh1, ret_w_in[i].astype(BF16), S, mode="rotary",
                              special_cols=2 * ret_heads * RET_DK, rope=rope, head_scale=RET_DK ** -0.5)
            y = _retention(proj, B, S, ret_heads)
            h = _proj_residual(y, ret_w_out[i].astype(BF16), h, g1, S)
        else:
            j = i - n_ret
            q = _nm_matmul(h, norm_gain[i, 0], sc1, sh1, fox_w_q[j].astype(BF16), S,
                           head_gain=fox_q_gain[j], special_cols=fox_width,
                           head_scale=LOG2E * FOX_DH ** -0.5)
            y = _fox_attention(q, kv_sh, decay_sh, B, S, fox_heads)
            h = _proj_residual(y, fox_w_out[j].astype(BF16), h, g1, S, tn=1024)
        if i % 2 == 0:
            h = _dense_ffn(h, norm_gain[i, 1], sc2, sh2, g2,
                           ffn_w13[i // 2].astype(BF16), ffn_w2[i // 2].astype(BF16), S)
        else:
            h = _moe_layer(h, norm_gain[i, 1], sc2, sh2, g2, moe_router[i // 2], moe_w13, moe_w2, i // 2, S)
        if i == n_ret - 1:
            ksh, ksc = vecs(kv_mod, 2)
            kv_sh = _nm_matmul(h, kv_norm_gain, ksc, ksh, kv_w[:, :2 * fox_width].astype(BF16), S,
                               head_gain=kv_k_gain, special_cols=fox_width)
            decay_sh = _forget_cumsum(h, kv_norm_gain, ksc, ksh, kv_w[:, 2 * fox_width:], kv_forget_bias, S)
    return h.reshape(B, S, D)
```

```python
import functools
import math

import jax
import jax.numpy as jnp
from jax import lax
from jax.experimental import pallas as pl
from jax.experimental.pallas import tpu as pltpu

F32 = jnp.float32
BF16 = jnp.bfloat16
HIGHEST = lax.Precision.HIGHEST

EPS = 1e-6
ROPE_BASE = 10000.0
RET_DK = 256
RET_DV = 2 * RET_DK
FOX_DH = 128
FOX_PIPELINE_LAG = 2
TOP_K = 2
LOG2E = math.log2(math.e)
LANES = 128
SUBLANES_BF16 = 16
VMEM_LIMIT_CAP = 56 * 1024 * 1024


def _params(semantics, vmem_bytes):
    limit = int(min(max(2 * vmem_bytes, 16 * 1024 * 1024), VMEM_LIMIT_CAP))
    return pltpu.CompilerParams(dimension_semantics=semantics, vmem_limit_bytes=limit)


def _tile(n, pref):
    t = min(n, pref)
    assert n % t == 0, (n, pref)
    return t


def _row_tile(d):
    assert d % (SUBLANES_BF16 * LANES) == 0, d
    return (d // LANES, LANES)


def _norm_mod(h, gain, scale, shift):
    ms = jnp.mean(h * h, axis=-1, keepdims=True)
    return h * lax.rsqrt(ms + EPS) * gain * (1.0 + scale) + shift


def _mod_kernel(c_ref, w_ref, b_ref, o_ref):
    c = c_ref[...]
    cond = c * jax.nn.sigmoid(c)
    o_ref[...] = jnp.dot(cond, w_ref[...], precision=HIGHEST, preferred_element_type=F32) + b_ref[...]


def _mod_call(c8, w, b):
    L, D, N = w.shape
    tn = _tile(N, 1024)
    return pl.pallas_call(
        _mod_kernel,
        grid=(L, N // tn),
        in_specs=[
            pl.BlockSpec((8, D), lambda l, j: (0, 0)),
            pl.BlockSpec((None, D, tn), lambda l, j: (l, 0, j)),
            pl.BlockSpec((None, 1, tn), lambda l, j: (l, 0, j)),
        ],
        out_specs=pl.BlockSpec((None, 8, tn), lambda l, j: (l, 0, j)),
        out_shape=jax.ShapeDtypeStruct((L, 8, N), F32),
        compiler_params=_params(("arbitrary", "arbitrary"), 2 * D * tn * 4),
        name="adaln_mod",
    )(c8, w, b.reshape(L, 1, N))


def _nm_matmul_kernel(h_ref, gain_ref, scale_ref, shift_ref, w_ref, aux_a_ref, aux_b_ref, o_ref, u_ref,
                      *, mode, n_special, head_scale):
    j = pl.program_id(1)

    @pl.when(j == 0)
    def _():
        u_ref[...] = _norm_mod(h_ref[...], gain_ref[...], scale_ref[...], shift_ref[...]).astype(BF16)

    acc = jnp.dot(u_ref[...], w_ref[...].astype(BF16), preferred_element_type=F32)

    if n_special == 0:
        o_ref[...] = acc.astype(o_ref.dtype)
        return

    @pl.when(j < n_special)
    def _():
        if mode == "head_norm":
            hg = aux_a_ref[...] * head_scale
            for c in range(acc.shape[1] // FOX_DH):
                blk = acc[:, c * FOX_DH:(c + 1) * FOX_DH]
                ms = jnp.mean(blk * blk, axis=-1, keepdims=True)
                o_ref[:, c * FOX_DH:(c + 1) * FOX_DH] = (blk * lax.rsqrt(ms + EPS) * hg).astype(o_ref.dtype)
        else:
            half = RET_DK // 2
            k_scale = jnp.where(j >= n_special // 2, head_scale, 1.0)
            cos, sin = aux_a_ref[...] * k_scale, aux_b_ref[...] * k_scale
            for c in range(acc.shape[1] // RET_DK):
                x1 = acc[:, c * RET_DK:c * RET_DK + half]
                x2 = acc[:, c * RET_DK + half:(c + 1) * RET_DK]
                o_ref[:, c * RET_DK:c * RET_DK + half] = (x1 * cos - x2 * sin).astype(o_ref.dtype)
                o_ref[:, c * RET_DK + half:(c + 1) * RET_DK] = (x1 * sin + x2 * cos).astype(o_ref.dtype)

    @pl.when(j >= n_special)
    def _():
        o_ref[...] = acc.astype(o_ref.dtype)


def _nm_matmul(h, gain, scale, shift, w, layer, seq, *, n_cols=None, mode="head_norm", special_cols=0,
               head_gain=None, rope=None, head_scale=1.0, tm=1024, tn=1024):
    T, D = h.shape
    N = w.shape[2] if n_cols is None else n_cols
    tm, tn = _tile(seq, tm), _tile(N, tn)
    assert special_cols % tn == 0
    rows_per_b = seq // tm
    if mode == "rotary":
        aux = rope
        aux_spec = pl.BlockSpec((tm, RET_DK // 2), lambda i, j: (i % rows_per_b, 0))
    else:
        gain_row = (jnp.ones((FOX_DH,), F32) if head_gain is None else head_gain).reshape(1, FOX_DH)
        aux = (gain_row, gain_row)
        aux_spec = pl.BlockSpec((1, FOX_DH), lambda i, j: (0, 0))
    vmem = 2 * tm * D * 4 + D * tn * (2 * 4 + 2) + 2 * tm * tn * 2 + tm * D * 2 + 4 * tm * LANES * 4
    kern = functools.partial(_nm_matmul_kernel, mode=mode, n_special=special_cols // tn,
                             head_scale=head_scale)
    return pl.pallas_call(
        kern,
        grid=(T // tm, N // tn),
        in_specs=[
            pl.BlockSpec((tm, D), lambda i, j: (i, 0)),
            pl.BlockSpec((1, D), lambda i, j: (0, 0)),
            pl.BlockSpec((None, 1, D), lambda i, j: (i // rows_per_b, 0, 0)),
            pl.BlockSpec((None, 1, D), lambda i, j: (i // rows_per_b, 0, 0)),
            pl.BlockSpec((None, D, tn), lambda i, j: (layer, 0, j)),
            aux_spec, aux_spec,
        ],
        out_specs=pl.BlockSpec((tm, tn), lambda i, j: (i, j)),
        out_shape=jax.ShapeDtypeStruct((T, N), BF16),
        scratch_shapes=[pltpu.VMEM((tm, D), BF16)],
        compiler_params=_params(("arbitrary", "arbitrary"), vmem),
        name="norm_mod_proj",
    )(h, gain.reshape(1, D), scale, shift, w, *aux)


def _proj_residual_kernel(y_ref, w_ref, h_ref, gate_ref, o_ref):
    acc = jnp.dot(y_ref[...], w_ref[...].astype(BF16), preferred_element_type=F32)
    o_ref[...] = h_ref[...] + gate_ref[...] * acc


def _proj_residual(y, w, layer, h, gate, seq, *, tm=1024, tn=512):
    T, K = y.shape
    D = w.shape[2]
    tm, tn = _tile(seq, tm), _tile(D, tn)
    rows_per_b = seq // tm
    vmem = 2 * tm * K * 2 + K * tn * (2 * 4 + 2) + 4 * tm * tn * 4
    return pl.pallas_call(
        _proj_residual_kernel,
        grid=(T // tm, D // tn),
        in_specs=[
            pl.BlockSpec((tm, K), lambda i, j: (i, 0)),
            pl.BlockSpec((None, K, tn), lambda i, j: (layer, 0, j)),
            pl.BlockSpec((tm, tn), lambda i, j: (i, j)),
            pl.BlockSpec((None, 1, tn), lambda i, j: (i // rows_per_b, 0, j)),
        ],
        out_specs=pl.BlockSpec((tm, tn), lambda i, j: (i, j)),
        out_shape=jax.ShapeDtypeStruct((T, D), F32),
        compiler_params=_params(("arbitrary", "arbitrary"), vmem),
        name="proj_residual",
    )(y, w, h, gate)


def _retention_kernel(lg_ref, q_ref, k_ref, v_ref, g_ref, o_ref, state_ref, *, chunk, n_chunks):
    @pl.when(pl.program_id(2) == 0)
    def _():
        state_ref[...] = jnp.zeros_like(state_ref)

    lg = lg_ref[pl.program_id(1)]
    row = lax.broadcasted_iota(jnp.int32, (chunk, chunk), 0)
    col = lax.broadcasted_iota(jnp.int32, (chunk, chunk), 1)
    dist = (row - col).astype(F32)
    d_local = jnp.where(dist >= 0, jnp.exp(lg * jnp.maximum(dist, 0.0)), 0.0)
    t = lax.broadcasted_iota(jnp.int32, (chunk, 1), 0).astype(F32)
    xi = jnp.exp(lg * (t + 1.0))
    zeta = jnp.exp(lg * (chunk - 1.0 - t))
    gamma_c = jnp.exp(jnp.full((1, 1), lg * chunk, F32))

    def body(ci, carry):
        r0 = pl.multiple_of(ci * chunk, chunk)
        q = q_ref[pl.ds(r0, chunk), :]
        k = k_ref[pl.ds(r0, chunk), :]
        v = v_ref[pl.ds(r0, chunk), :]
        scores = lax.dot_general(q, k, (((1,), (1,)), ((), ())), preferred_element_type=F32) * d_local
        inner = jnp.dot(scores.astype(BF16), v, preferred_element_type=F32)
        state = state_ref[...]
        cross = jnp.dot(q, state.astype(BF16), preferred_element_type=F32) * xi
        vz = (v.astype(F32) * zeta).astype(BF16)
        state_ref[...] = gamma_c * state + lax.dot_general(
            k, vz, (((0,), (0,)), ((), ())), preferred_element_type=F32)
        o = inner + cross
        ms = jnp.mean(o * o, axis=-1, keepdims=True)
        g = g_ref[pl.ds(r0, chunk), :].astype(F32)
        o_ref[pl.ds(r0, chunk), :] = (g * jax.nn.sigmoid(g) * (o * lax.rsqrt(ms + EPS))).astype(o_ref.dtype)
        return carry

    lax.fori_loop(0, n_chunks, body, 0, unroll=True)


def _retention(proj, batch, seq, heads, *, chunk=256, rows=1024):
    T = proj.shape[0]
    chunk = _tile(seq, chunk)
    rows = _tile(seq, rows)
    nblk = seq // rows
    log_gamma = jnp.log1p(-jnp.exp2(-5.0 - jnp.arange(heads, dtype=F32)))
    kern = functools.partial(_retention_kernel, chunk=chunk, n_chunks=rows // chunk)
    qk_blocks = heads
    v_off = 2 * heads * RET_DK // RET_DV
    return pl.pallas_call(
        kern,
        grid=(batch, heads, nblk),
        in_specs=[
            pl.BlockSpec(memory_space=pltpu.SMEM),
            pl.BlockSpec((rows, RET_DK), lambda b, h, c: (b * nblk + c, h)),
            pl.BlockSpec((rows, RET_DK), lambda b, h, c: (b * nblk + c, qk_blocks + h)),
            pl.BlockSpec((rows, RET_DV), lambda b, h, c: (b * nblk + c, v_off + h)),
            pl.BlockSpec((rows, RET_DV), lambda b, h, c: (b * nblk + c, v_off + heads + h)),
        ],
        out_specs=pl.BlockSpec((rows, RET_DV), lambda b, h, c: (b * nblk + c, h)),
        out_shape=jax.ShapeDtypeStruct((T, heads * RET_DV), BF16),
        scratch_shapes=[pltpu.VMEM((RET_DK, RET_DV), F32)],
        compiler_params=_params(("arbitrary", "arbitrary", "arbitrary"), 16 * 1024 * 1024),
        name="retention",
    )(log_gamma, proj, proj, proj, proj)


def _bf16_pieces(x, n):
    pieces = []
    for _ in range(n):
        p = x.astype(BF16)
        pieces.append(p)
        x = x - p.astype(F32)
    return pieces


def _small_proj(h_ref, gain_ref, scale_ref, shift_ref, w_ref):
    u = _norm_mod(h_ref[...], gain_ref[...], scale_ref[...], shift_ref[...])
    u_hi, u_lo = _bf16_pieces(u, 2)
    w_hi, w_lo = _bf16_pieces(w_ref[...], 2)
    out = (jnp.dot(u_hi, w_hi, preferred_element_type=F32) + jnp.dot(u_hi, w_lo, preferred_element_type=F32)
           + jnp.dot(u_lo, w_hi, preferred_element_type=F32))
    return u, out


def _forget_kernel(h_ref, gain_ref, scale_ref, shift_ref, w_ref, bias_ref, o_ref, carry_ref,
                   *, tiles_per_seq, n_heads):
    @pl.when(pl.program_id(0) % tiles_per_seq == 0)
    def _():
        carry_ref[...] = jnp.zeros_like(carry_ref)

    _, f = _small_proj(h_ref, gain_ref, scale_ref, shift_ref, w_ref)
    log_f = jax.nn.log_sigmoid(f + bias_ref[...])
    tm = log_f.shape[0]
    row = lax.broadcasted_iota(jnp.int32, (tm, tm), 0)
    col = lax.broadcasted_iota(jnp.int32, (tm, tm), 1)
    tri = jnp.where(col <= row, 1.0, 0.0).astype(BF16)
    cum = sum(jnp.dot(tri, piece, preferred_element_type=F32) for piece in _bf16_pieces(log_f, 3))
    cum = cum + carry_ref[...]
    carry_ref[...] = cum[tm - 1:tm, :]
    neg = cum * (-LOG2E)
    lane = lax.broadcasted_iota(jnp.int32, (tm, FOX_DH), 1)
    for hh in range(n_heads):
        b = neg[:, hh:hh + 1]
        hi = b.astype(BF16).astype(F32)
        mid = (b - hi).astype(BF16).astype(F32)
        lo = b - hi - mid
        blk = jnp.where(lane == 0, hi, jnp.where(lane == 1, mid, jnp.where(lane == 2, lo, 0.0)))
        o_ref[:, hh * FOX_DH:(hh + 1) * FOX_DH] = blk.astype(o_ref.dtype)


def _forget_cumsum(h, gain, scale, shift, w_f, bias, seq, *, tm=512):
    T, D = h.shape
    nh = w_f.shape[1]
    tm = _tile(seq, tm)
    rows_per_b = seq // tm
    w_pad = jnp.zeros((D, LANES), F32).at[:, :nh].set(w_f)
    b_pad = jnp.zeros((1, LANES), F32).at[0, :nh].set(bias)
    kern = functools.partial(_forget_kernel, tiles_per_seq=rows_per_b, n_heads=nh)
    return pl.pallas_call(
        kern,
        grid=(T // tm,),
        in_specs=[
            pl.BlockSpec((tm, D), lambda i: (i, 0)),
            pl.BlockSpec((1, D), lambda i: (0, 0)),
            pl.BlockSpec((None, 1, D), lambda i: (i // rows_per_b, 0, 0)),
            pl.BlockSpec((None, 1, D), lambda i: (i // rows_per_b, 0, 0)),
            pl.BlockSpec((D, LANES), lambda i: (0, 0)),
            pl.BlockSpec((1, LANES), lambda i: (0, 0)),
        ],
        out_specs=pl.BlockSpec((tm, nh * FOX_DH), lambda i: (i, 0)),
        out_shape=jax.ShapeDtypeStruct((T, nh * FOX_DH), BF16),
        scratch_shapes=[pltpu.VMEM((1, LANES), F32)],
        compiler_params=_params(("arbitrary",), 2 * tm * D * 4 + 4 * tm * D * 4),
        name="forget_cumsum",
    )(h, gain.reshape(1, D), scale, shift, w_pad, b_pad)


def _router_kernel(h_ref, gain_ref, scale_ref, shift_ref, w_ref, u_ref, route_ref, w0_ref, w1_ref,
                   count_ref, carry_ref, *, n_experts):
    @pl.when(pl.program_id(0) == 0)
    def _():
        carry_ref[...] = jnp.zeros_like(carry_ref)

    u, logits = _small_proj(h_ref, gain_ref, scale_ref, shift_ref, w_ref)
    u_ref[...] = u.reshape(u_ref.shape).astype(u_ref.dtype)
    tm = logits.shape[0]
    lane = lax.broadcasted_iota(jnp.int32, (tm, LANES), 1)
    neg = -jnp.inf
    l0 = jnp.where(lane < n_experts, logits, neg)
    m0 = jnp.max(l0, axis=-1, keepdims=True)
    i0 = jnp.min(jnp.where(l0 == m0, lane, LANES), axis=-1, keepdims=True)
    l1 = jnp.where(lane == i0, neg, l0)
    m1 = jnp.max(l1, axis=-1, keepdims=True)
    i1 = jnp.min(jnp.where(l1 == m1, lane, LANES), axis=-1, keepdims=True)
    e = jnp.exp(m1 - m0)
    wgt0 = 1.0 / (1.0 + e)
    wgt1 = e / (1.0 + e)
    onehot = jnp.where(lane == i0, 1.0, jnp.where(lane == i1, 1.0, 0.0))
    row = lax.broadcasted_iota(jnp.int32, (tm, tm), 0)
    col = lax.broadcasted_iota(jnp.int32, (tm, tm), 1)
    tri = jnp.where(col < row, 1.0, 0.0).astype(BF16)
    before = jnp.dot(tri, onehot.astype(BF16), preferred_element_type=F32) + carry_ref[...]
    rank0 = jnp.sum(jnp.where(lane == i0, before, 0.0), axis=-1, keepdims=True)
    rank1 = jnp.sum(jnp.where(lane == i1, before, 0.0), axis=-1, keepdims=True)
    total = carry_ref[...] + jnp.sum(onehot, axis=0, keepdims=True)
    carry_ref[...] = total
    count_ref[...] = jnp.broadcast_to(total, count_ref.shape)
    route = jnp.where(lane == 0, i0.astype(F32),
                      jnp.where(lane == 1, i1.astype(F32),
                                jnp.where(lane == 2, rank0, jnp.where(lane == 3, rank1, 0.0))))
    route_ref[...] = route
    w0_ref[...] = jnp.broadcast_to(wgt0, w0_ref.shape)
    w1_ref[...] = jnp.broadcast_to(wgt1, w1_ref.shape)


def _router(h, gain, scale, shift, w_router, seq, *, tm=512):
    T, D = h.shape
    E = w_router.shape[1]
    tm = _tile(seq, tm)
    rows_per_b = seq // tm
    w_pad = jnp.zeros((D, LANES), F32).at[:, :E].set(w_router)
    kern = functools.partial(_router_kernel, n_experts=E)
    row_spec = pl.BlockSpec((tm, LANES), lambda i: (i, 0))
    row_shape = jax.ShapeDtypeStruct((T, LANES), F32)
    return pl.pallas_call(
        kern,
        grid=(T // tm,),
        in_specs=[
            pl.BlockSpec((tm, D), lambda i: (i, 0)),
            pl.BlockSpec((1, D), lambda i: (0, 0)),
            pl.BlockSpec((None, 1, D), lambda i: (i // rows_per_b, 0, 0)),
            pl.BlockSpec((None, 1, D), lambda i: (i // rows_per_b, 0, 0)),
            pl.BlockSpec((D, LANES), lambda i: (0, 0)),
        ],
        out_specs=[
            pl.BlockSpec((tm,) + _row_tile(D), lambda i: (i, 0, 0)),
            row_spec, row_spec, row_spec,
            pl.BlockSpec((8, LANES), lambda i: (0, 0)),
        ],
        out_shape=[
            jax.ShapeDtypeStruct((T,) + _row_tile(D), BF16),
            row_shape, row_shape, row_shape,
            jax.ShapeDtypeStruct((8, LANES), F32),
        ],
        scratch_shapes=[pltpu.VMEM((1, LANES), F32)],
        compiler_params=_params(("arbitrary",), 2 * tm * D * 4 + 2 * tm * D * 2 + 4 * tm * D * 4),
        name="moe_router",
    )(h, gain.reshape(1, D), scale, shift, w_pad)


def _fox_kernel(ti_ref, tj_ref, q_ref, k_ref, e_ref, v_ref, o_ref, kp_ref, vp_ref, m_ref, acc_ref,
                *bufs, tq, n_off, n_diag):
    lane = lax.broadcasted_iota(jnp.int32, k_ref.shape, 1)
    kp_ref[:FOX_DH, :] = k_ref[...].T
    kp_ref[FOX_DH:, :] = e_ref[...].T
    vp_ref[:, :FOX_DH] = v_ref[...]
    vp_ref[:, FOX_DH:] = jnp.where(lane == 0, 1.0, 0.0).astype(BF16)
    m_ref[...] = jnp.full(m_ref.shape, -jnp.inf, F32)
    acc_ref[...] = jnp.zeros_like(acc_ref)

    LAG = FOX_PIPELINE_LAG
    nbuf = 2 * LAG
    lane = lax.broadcasted_iota(jnp.int32, (tq, FOX_DH), 1)
    ones = jnp.where(lane < 3, 1.0, 0.0).astype(BF16)
    s_refs, p_refs, a_refs = bufs[:nbuf], bufs[nbuf:2 * nbuf], bufs[2 * nbuf:]

    def tile_rows(idx_ref, t):
        return pl.ds(pl.multiple_of(idx_ref[t] * tq, tq), tq)

    def logits(t, par):
        qp = jnp.concatenate([q_ref[tile_rows(ti_ref, t), :], ones], axis=-1)
        kp = kp_ref[:, tile_rows(tj_ref, t)]
        s_refs[par][...] = jnp.dot(qp, kp, preferred_element_type=F32)

    def softmax(t, par, diag):
        rows = tile_rows(ti_ref, t)
        s = s_refs[par][...]
        if diag:
            row = lax.broadcasted_iota(jnp.int32, (tq, tq), 0)
            col = lax.broadcasted_iota(jnp.int32, (tq, tq), 1)
            s = jnp.where(col <= row, s, -jnp.inf)
        m = m_ref[rows, :]
        m_new = jnp.maximum(m, jnp.max(s, axis=-1, keepdims=True))
        m_ref[rows, :] = m_new
        a_refs[par][...] = jnp.exp2(m - m_new)
        p_refs[par][...] = jnp.exp2(s - m_new).astype(BF16)

    def values(t, par, diag):
        rows = tile_rows(ti_ref, t)
        vp = vp_ref[tile_rows(tj_ref, t), :]
        acc = a_refs[par][...] * acc_ref[rows, :] + jnp.dot(p_refs[par][...], vp,
                                                            preferred_element_type=F32)
        if diag:
            o_ref[rows, :] = (acc[:, :FOX_DH] / acc[:, FOX_DH:FOX_DH + 1]).astype(o_ref.dtype)
        else:
            acc_ref[rows, :] = acc

    n = n_off + n_diag
    last = n - 1 + 2 * LAG

    def plan(u):
        tv, ts = u - 2 * LAG, u - LAG
        return (0 <= tv < n, tv >= n_off), (0 <= ts < n, ts >= n_off), u < n

    def step(u, slot, stages):
        (do_v, diag_v), (do_s, diag_s), do_l = stages
        if do_v:
            values(u - 2 * LAG, slot, diag_v)
        if do_s:
            softmax(u - LAG, (slot + LAG) % nbuf, diag_s)
        if do_l:
            logits(u, slot)

    def uniform(u0):
        plans = {plan(u0 + g) for g in range(nbuf)}
        stages = next(iter(plans))
        full = stages[0][0] and stages[1][0] and stages[2]
        return stages if len(plans) == 1 and full and u0 + nbuf - 1 <= last else None

    u = 0
    while u <= last:
        stages = uniform(u) if u % nbuf == 0 else None
        if stages is None:
            step(u, u % nbuf, plan(u))
            u += 1
            continue
        count = 1
        while uniform(u + count * nbuf) == stages:
            count += 1

        def group(d, c, base=u, stages=stages):
            for g in range(nbuf):
                step(base + nbuf * d + g, g, stages)
            return c

        lax.fori_loop(0, count, group, 0)
        u += count * nbuf


def _fox_attention(q, kv, e, batch, seq, heads, *, tq=512):
    T = q.shape[0]
    tq = _tile(seq, tq)
    nq = seq // tq
    below = [(i, j) for i in range(nq) for j in range(i)]
    tiles = below + [(i, i) for i in range(nq)]
    ti = jnp.array([t[0] for t in tiles], jnp.int32)
    tj = jnp.array([t[1] for t in tiles], jnp.int32)
    kern = functools.partial(_fox_kernel, tq=tq, n_off=len(below), n_diag=nq)
    nbuf = 2 * FOX_PIPELINE_LAG
    grid_spec = pltpu.PrefetchScalarGridSpec(
        num_scalar_prefetch=2,
        grid=(batch, heads),
        in_specs=[
            pl.BlockSpec((seq, FOX_DH), lambda b, h, ti, tj: (b, h)),
            pl.BlockSpec((seq, FOX_DH), lambda b, h, ti, tj: (b, h)),
            pl.BlockSpec((seq, FOX_DH), lambda b, h, ti, tj: (b, h)),
            pl.BlockSpec((seq, FOX_DH), lambda b, h, ti, tj: (b, heads + h)),
        ],
        out_specs=pl.BlockSpec((seq, FOX_DH), lambda b, h, ti, tj: (b, h)),
        scratch_shapes=([pltpu.VMEM((2 * FOX_DH, seq), BF16), pltpu.VMEM((seq, 2 * FOX_DH), BF16),
                         pltpu.VMEM((seq, 1), F32), pltpu.VMEM((seq, 2 * FOX_DH), F32)]
                        + [pltpu.VMEM((tq, tq), F32)] * nbuf + [pltpu.VMEM((tq, tq), BF16)] * nbuf
                        + [pltpu.VMEM((tq, 1), F32)] * nbuf),
    )
    vmem = 14 * seq * FOX_DH * 2 + seq * (LANES + 2 * FOX_DH) * 4 + nbuf * tq * (6 * tq + 4 * LANES)
    return pl.pallas_call(
        kern,
        grid_spec=grid_spec,
        out_shape=jax.ShapeDtypeStruct((T, heads * FOX_DH), BF16),
        compiler_params=_params(("arbitrary", "arbitrary"), vmem),
        name="fox_attention",
    )(ti, tj, q, kv, e, kv)


def _swiglu_step(u, w1_ref, w3_ref, w2_ref, acc_ref):
    a = jnp.dot(u, w1_ref[...].astype(BF16), preferred_element_type=F32)
    g = jnp.dot(u, w3_ref[...].astype(BF16), preferred_element_type=F32)
    mid = (a * jax.nn.sigmoid(a) * g).astype(BF16)
    acc_ref[...] += jnp.dot(mid, w2_ref[...].astype(BF16), preferred_element_type=F32)


def _dense_ffn_kernel(h_ref, gain_ref, scale_ref, shift_ref, gate_ref, w1_ref, w3_ref, w2_ref,
                      o_ref, u_ref, acc_ref):
    f = pl.program_id(1)

    @pl.when(f == 0)
    def _():
        u_ref[...] = _norm_mod(h_ref[...], gain_ref[...], scale_ref[...], shift_ref[...]).astype(BF16)
        acc_ref[...] = jnp.zeros_like(acc_ref)

    _swiglu_step(u_ref[...], w1_ref, w3_ref, w2_ref, acc_ref)

    @pl.when(f == pl.num_programs(1) - 1)
    def _():
        o_ref[...] = h_ref[...] + gate_ref[...] * acc_ref[...]


def _dense_ffn(h, gain, scale, shift, gate, w13, w2, seq, *, tm=512, tf=512):
    T, D = h.shape
    F = w2.shape[0]
    tm, tf = _tile(seq, tm), _tile(F, tf)
    nf = F // tf
    rows_per_b = seq // tm
    vec = pl.BlockSpec((None, 1, D), lambda i, f: (i // rows_per_b, 0, 0))
    vmem = 4 * tm * D * 4 + tm * D * 2 + tm * D * 4 + 2 * 3 * D * tf * 2
    return pl.pallas_call(
        _dense_ffn_kernel,
        grid=(T // tm, nf),
        in_specs=[
            pl.BlockSpec((tm, D), lambda i, f: (i, 0)),
            pl.BlockSpec((1, D), lambda i, f: (0, 0)),
            vec, vec, vec,
            pl.BlockSpec((D, tf), lambda i, f: (0, f)),
            pl.BlockSpec((D, tf), lambda i, f: (0, nf + f)),
            pl.BlockSpec((tf, D), lambda i, f: (f, 0)),
        ],
        out_specs=pl.BlockSpec((tm, D), lambda i, f: (i, 0)),
        out_shape=jax.ShapeDtypeStruct((T, D), F32),
        scratch_shapes=[pltpu.VMEM((tm, D), BF16), pltpu.VMEM((tm, D), F32)],
        compiler_params=_params(("arbitrary", "arbitrary"), vmem),
        name="dense_swiglu",
    )(h, gain.reshape(1, D), scale, shift, gate, w13, w13, w2)


def _expert_ffn_kernel(te_ref, tr_ref, x_ref, w1_ref, w3_ref, w2_ref, o_ref, x2_ref, acc_ref):
    i, f = pl.program_id(0), pl.program_id(1)
    rows = tr_ref[i]
    quarter = x2_ref.shape[0] // 4

    @pl.when(f == 0)
    def _():
        x2_ref[...] = x_ref[...].reshape(x2_ref.shape)
        acc_ref[...] = jnp.zeros_like(acc_ref)

    for q in range(1, 5):
        @pl.when(jnp.logical_and(rows > (q - 1) * quarter, rows <= q * quarter))
        def _(n=q * quarter):
            _swiglu_step(x2_ref[:n, :], w1_ref, w3_ref, w2_ref, acc_ref.at[pl.ds(0, n), :])

    @pl.when(f == pl.num_programs(1) - 1)
    def _():
        o_ref[...] = acc_ref[...].astype(o_ref.dtype).reshape(o_ref.shape)


def _expert_ffn(xs, tile_expert, tile_rows, w13, w2, layer, *, tm, tf=256):
    P = xs.shape[0]
    row_tile = xs.shape[1:]
    _, E, F, D = w2.shape
    tf = _tile(F, tf)
    nf = F // tf

    def fidx(i, f, nu):
        return jnp.where(nu[i] > 0, f, nf - 1)

    grid_spec = pltpu.PrefetchScalarGridSpec(
        num_scalar_prefetch=2,
        grid=(P // tm, nf),
        in_specs=[
            pl.BlockSpec((tm,) + row_tile, lambda i, f, te, nu: (i, 0, 0)),
            pl.BlockSpec((None, None, D, tf), lambda i, f, te, nu: (layer, te[i], 0, fidx(i, f, nu))),
            pl.BlockSpec((None, None, D, tf), lambda i, f, te, nu: (layer, te[i], 0, nf + fidx(i, f, nu))),
            pl.BlockSpec((None, None, tf, D), lambda i, f, te, nu: (layer, te[i], fidx(i, f, nu), 0)),
        ],
        out_specs=pl.BlockSpec((tm,) + row_tile, lambda i, f, te, nu: (i, 0, 0)),
        scratch_shapes=[pltpu.VMEM((tm, D), BF16), pltpu.VMEM((tm, D), F32)],
    )
    w_bytes = jnp.dtype(w2.dtype).itemsize
    vmem = 5 * tm * D * 2 + tm * D * 4 + 3 * D * tf * (2 * w_bytes + 2) + 4 * tm * tf * 4
    return pl.pallas_call(
        _expert_ffn_kernel,
        grid_spec=grid_spec,
        out_shape=jax.ShapeDtypeStruct((P,) + row_tile, BF16),
        compiler_params=_params(("arbitrary", "arbitrary"), vmem),
        name="expert_swiglu",
    )(tile_expert, tile_rows, xs, w13, w13, w2)


def _dispatch_kernel(pos_ref, u_ref, xs_in_ref, xs_ref, sem, *, chunk):
    del xs_in_ref

    def copy(t, k):
        return pltpu.make_async_copy(u_ref.at[t], xs_ref.at[pos_ref[TOP_K * t + k]], sem)

    def start(t, c):
        for k in range(TOP_K):
            copy(t, k).start()
        return c

    def wait(t, c):
        for k in range(TOP_K):
            copy(t, k).wait()
        return c

    lax.fori_loop(0, chunk, start, 0, unroll=4)
    lax.fori_loop(0, chunk, wait, 0, unroll=4)


def _dispatch(u3, pos_flat, n_slots, *, chunk=512):
    T = u3.shape[0]
    chunk = _tile(T, chunk)
    xs0 = jnp.zeros((n_slots,) + u3.shape[1:], u3.dtype)
    kern = functools.partial(_dispatch_kernel, chunk=chunk)
    return pl.pallas_call(
        kern,
        grid=(T // chunk,),
        in_specs=[
            pl.BlockSpec((TOP_K * chunk,), lambda i: (i,), memory_space=pltpu.SMEM),
            pl.BlockSpec((chunk,) + u3.shape[1:], lambda i: (i, 0, 0)),
            pl.BlockSpec(memory_space=pl.ANY),
        ],
        out_specs=pl.BlockSpec(memory_space=pl.ANY),
        out_shape=jax.ShapeDtypeStruct(xs0.shape, xs0.dtype),
        scratch_shapes=[pltpu.SemaphoreType.DMA(())],
        input_output_aliases={2: 0},
        compiler_params=_params(("arbitrary",), 4 * chunk * u3.shape[1] * u3.shape[2] * 2),
        name="moe_dispatch",
    )(pos_flat, u3, xs0)


def _combine_kernel(pos_ref, ys_ref, w0_ref, w1_ref, h_ref, gate_ref, o_ref, buf_ref, sem, *, chunk):
    def copy(t, k):
        return pltpu.make_async_copy(ys_ref.at[pos_ref[TOP_K * t + k]], buf_ref.at[k, t], sem)

    def start(t, c):
        for k in range(TOP_K):
            copy(t, k).start()
        return c

    def wait(t, c):
        for k in range(TOP_K):
            copy(t, k).wait()
        return c

    lax.fori_loop(0, chunk, start, 0, unroll=4)
    lax.fori_loop(0, chunk, wait, 0, unroll=4)
    y = w0_ref[...] * buf_ref[0].astype(F32) + w1_ref[...] * buf_ref[1].astype(F32)
    o_ref[...] = h_ref[...] + gate_ref[...] * y.reshape(o_ref.shape)


def _combine(ys3, pos_flat, w0, w1, h, gate, seq, *, chunk=512):
    T, D = h.shape
    tile = ys3.shape[1:]
    chunk = _tile(seq, chunk)
    rows_per_b = seq // chunk
    kern = functools.partial(_combine_kernel, chunk=chunk)
    wspec = pl.BlockSpec((chunk, 1, LANES), lambda i: (i, 0, 0))
    return pl.pallas_call(
        kern,
        grid=(T // chunk,),
        in_specs=[
            pl.BlockSpec((TOP_K * chunk,), lambda i: (i,), memory_space=pltpu.SMEM),
            pl.BlockSpec(memory_space=pl.ANY),
            wspec, wspec,
            pl.BlockSpec((chunk, D), lambda i: (i, 0)),
            pl.BlockSpec((None, 1, D), lambda i: (i // rows_per_b, 0, 0)),
        ],
        out_specs=pl.BlockSpec((chunk, D), lambda i: (i, 0)),
        out_shape=jax.ShapeDtypeStruct((T, D), F32),
        scratch_shapes=[pltpu.VMEM((TOP_K, chunk) + tile, BF16), pltpu.SemaphoreType.DMA(())],
        compiler_params=_params(("arbitrary",), 10 * chunk * D * 4),
        name="moe_combine",
    )(pos_flat, ys3, w0.reshape(T, 1, LANES), w1.reshape(T, 1, LANES), h, gate)


def _moe_layer(h, gain, scale, shift, gate, w_router, w13, w2, layer, seq, *, tm=1024):
    T, D = h.shape
    E = w_router.shape[1]
    u3, route, w0, w1, counts = _router(h, gain, scale, shift, w_router, seq)

    expert = route[:, :TOP_K].astype(jnp.int32)
    rank = route[:, TOP_K:2 * TOP_K].astype(jnp.int32)
    count = counts[0, :E].astype(jnp.int32)
    padded = ((count + tm - 1) // tm) * tm
    ends = jnp.cumsum(padded)
    starts = ends - padded
    pos_flat = (starts[expert] + rank).reshape(-1)
    n_tiles = (TOP_K * T) // tm + E
    tile_start = jnp.arange(n_tiles, dtype=jnp.int32) * tm
    last_used = jnp.maximum(ends[-1] // tm - 1, 0)
    tile_expert = jnp.minimum(jnp.searchsorted(ends, tile_start, side="right"), E - 1).astype(jnp.int32)
    tile_rows = jnp.clip(starts[tile_expert] + count[tile_expert] - tile_start, 0, tm).astype(jnp.int32)
    tile_rows = jnp.where(tile_start < ends[-1], tile_rows, 0)
    tile_expert = jnp.where(tile_start < ends[-1], tile_expert, tile_expert[last_used])

    xs3 = _dispatch(u3, pos_flat, n_tiles * tm)
    ys3 = _expert_ffn(xs3, tile_expert, tile_rows, w13, w2, layer, tm=tm)
    return _combine(ys3, pos_flat, w0, w1, h, gate, seq)


def kernel(x, c, norm_gain, mod_w, mod_b, ret_w_in, ret_w_out, fox_w_q, fox_q_gain, fox_w_out,
           kv_norm_gain, kv_mod_w, kv_mod_b, kv_w, kv_forget_bias, kv_k_gain,
           ffn_w13, ffn_w2, moe_router, moe_w13, moe_w2):
    B, S, D = x.shape
    T = B * S
    depth = mod_w.shape[0]
    n_ret = ret_w_in.shape[0]
    ret_heads = ret_w_out.shape[1] // RET_DV
    fox_heads = fox_w_q.shape[2] // FOX_DH
    fox_width = fox_heads * FOX_DH

    c8 = jnp.zeros((8, D), F32).at[:B].set(c)
    mod = _mod_call(c8, mod_w, mod_b)[:, :B]
    kv_mod = _mod_call(c8, kv_mod_w[None], kv_mod_b[None])[0, :B]

    def vecs(m, n):
        return [v.reshape(B, 1, D) for v in jnp.split(m, n, axis=-1)]

    half = RET_DK // 2
    inv = 1.0 / (ROPE_BASE ** (jnp.arange(half, dtype=F32) / half))
    ang = jnp.arange(S, dtype=F32)[:, None] * inv[None, :]
    rope = (jnp.cos(ang), jnp.sin(ang))

    h = x.reshape(T, D)
    kv_sh = decay_sh = None
    for i in range(depth):
        sh1, sc1, g1, sh2, sc2, g2 = vecs(mod[i], 6)
        if i < n_ret:
            proj = _nm_matmul(h, norm_gain[i, 0], sc1, sh1, ret_w_in, i, S, mode="rotary",
                              special_cols=2 * ret_heads * RET_DK, rope=rope, head_scale=RET_DK ** -0.5)
            y = _retention(proj, B, S, ret_heads)
            h = _proj_residual(y, ret_w_out, i, h, g1, S)
        else:
            j = i - n_ret
            q = _nm_matmul(h, norm_gain[i, 0], sc1, sh1, fox_w_q, j, S,
                           head_gain=fox_q_gain[j], special_cols=fox_width,
                           head_scale=LOG2E * FOX_DH ** -0.5)
            y = _fox_attention(q, kv_sh, decay_sh, B, S, fox_heads)
            h = _proj_residual(y, fox_w_out, j, h, g1, S, tn=1024)
        if i % 2 == 0:
            h = _dense_ffn(h, norm_gain[i, 1], sc2, sh2, g2,
                           ffn_w13[i // 2].astype(BF16), ffn_w2[i // 2].astype(BF16), S)
        else:
            h = _moe_layer(h, norm_gain[i, 1], sc2, sh2, g2, moe_router[i // 2], moe_w13, moe_w2, i // 2, S)
        if i == n_ret - 1:
            ksh, ksc = vecs(kv_mod, 2)
            kv_sh = _nm_matmul(h, kv_norm_gain, ksc, ksh, kv_w[None], 0, S, n_cols=2 * fox_width,
                               head_gain=kv_k_gain, special_cols=fox_width)
            decay_sh = _forget_cumsum(h, kv_norm_gain, ksc, ksh, kv_w[:, 2 * fox_width:], kv_forget_bias, S)
    return h.reshape(B, S, D)
```

```python
import functools
import math

import jax
import jax.numpy as jnp
from jax import lax
from jax.experimental import pallas as pl
from jax.experimental.pallas import tpu as pltpu

F32 = jnp.float32
BF16 = jnp.bfloat16
HIGHEST = lax.Precision.HIGHEST

EPS = 1e-6
ROPE_BASE = 10000.0
RET_DK = 256
RET_DV = 2 * RET_DK
FOX_DH = 128
FOX_PIPELINE_LAG = 2
TOP_K = 2
LOG2E = math.log2(math.e)
LANES = 128
SUBLANES_BF16 = 16
VMEM_LIMIT_CAP = 56 * 1024 * 1024


def _params(semantics, vmem_bytes):
    limit = int(min(max(2 * vmem_bytes, 16 * 1024 * 1024), VMEM_LIMIT_CAP))
    return pltpu.CompilerParams(dimension_semantics=semantics, vmem_limit_bytes=limit)


def _tile(n, pref):
    t = min(n, pref)
    assert n % t == 0, (n, pref)
    return t


def _row_tile(d):
    assert d % (SUBLANES_BF16 * LANES) == 0, d
    return (d // LANES, LANES)


def _norm_mod(h, gain, scale, shift):
    ms = jnp.mean(h * h, axis=-1, keepdims=True)
    return h * lax.rsqrt(ms + EPS) * (gain * (1.0 + scale)) + shift


def _mod_kernel(c_ref, w_ref, b_ref, o_ref):
    c = c_ref[...]
    cond = c * jax.nn.sigmoid(c)
    o_ref[...] = jnp.dot(cond, w_ref[...], precision=HIGHEST, preferred_element_type=F32) + b_ref[...]


def _mod_call(c8, w, b):
    L, D, N = w.shape
    tn = _tile(N, 1024)
    return pl.pallas_call(
        _mod_kernel,
        grid=(L, N // tn),
        in_specs=[
            pl.BlockSpec((8, D), lambda l, j: (0, 0)),
            pl.BlockSpec((None, D, tn), lambda l, j: (l, 0, j)),
            pl.BlockSpec((None, 1, tn), lambda l, j: (l, 0, j)),
        ],
        out_specs=pl.BlockSpec((None, 8, tn), lambda l, j: (l, 0, j)),
        out_shape=jax.ShapeDtypeStruct((L, 8, N), F32),
        compiler_params=_params(("arbitrary", "arbitrary"), 2 * D * tn * 4),
        name="adaln_mod",
    )(c8, w, b.reshape(L, 1, N))


def _nm_matmul_kernel(h_ref, gain_ref, scale_ref, shift_ref, w_ref, hg_ref, o_ref, u_ref,
                      *, n_special, head_scale):
    j = pl.program_id(1)

    @pl.when(j == 0)
    def _():
        u_ref[...] = _norm_mod(h_ref[...], gain_ref[...], scale_ref[...], shift_ref[...]).astype(BF16)

    acc = jnp.dot(u_ref[...], w_ref[...], preferred_element_type=F32)

    if n_special == 0:
        o_ref[...] = acc.astype(o_ref.dtype)
        return

    @pl.when(j < n_special)
    def _():
        hg = hg_ref[...] * head_scale
        for c in range(acc.shape[1] // FOX_DH):
            blk = acc[:, c * FOX_DH:(c + 1) * FOX_DH]
            ms = jnp.mean(blk * blk, axis=-1, keepdims=True)
            o_ref[:, c * FOX_DH:(c + 1) * FOX_DH] = (blk * lax.rsqrt(ms + EPS) * hg).astype(o_ref.dtype)

    @pl.when(j >= n_special)
    def _():
        o_ref[...] = acc.astype(o_ref.dtype)


def _nm_matmul(h, gain, scale, shift, w, seq, *, special_cols=0, head_gain=None, head_scale=1.0,
               tm=1024, tn=1024):
    T, D = h.shape
    N = w.shape[1]
    tm, tn = _tile(seq, tm), _tile(N, tn)
    assert special_cols % tn == 0
    rows_per_b = seq // tm
    gain_row = (jnp.ones((FOX_DH,), F32) if head_gain is None else head_gain).reshape(1, FOX_DH)
    vmem = 2 * tm * D * 4 + 2 * D * tn * 2 + 2 * tm * tn * 2 + tm * D * 2
    kern = functools.partial(_nm_matmul_kernel, n_special=special_cols // tn, head_scale=head_scale)
    return pl.pallas_call(
        kern,
        grid=(T // tm, N // tn),
        in_specs=[
            pl.BlockSpec((tm, D), lambda i, j: (i, 0)),
            pl.BlockSpec((1, D), lambda i, j: (0, 0)),
            pl.BlockSpec((None, 1, D), lambda i, j: (i // rows_per_b, 0, 0)),
            pl.BlockSpec((None, 1, D), lambda i, j: (i // rows_per_b, 0, 0)),
            pl.BlockSpec((D, tn), lambda i, j: (0, j)),
            pl.BlockSpec((1, FOX_DH), lambda i, j: (0, 0)),
        ],
        out_specs=pl.BlockSpec((tm, tn), lambda i, j: (i, j)),
        out_shape=jax.ShapeDtypeStruct((T, N), BF16),
        scratch_shapes=[pltpu.VMEM((tm, D), BF16)],
        compiler_params=_params(("arbitrary", "arbitrary"), vmem),
        name="norm_mod_proj",
    )(h, gain.reshape(1, D), scale, shift, w, gain_row)


def _proj_residual_kernel(y_ref, w_ref, h_ref, gate_ref, o_ref):
    acc = jnp.dot(y_ref[...], w_ref[...], preferred_element_type=F32)
    o_ref[...] = h_ref[...] + gate_ref[...] * acc


def _proj_residual(y, w, h, gate, seq, *, tm=1024, tn=512):
    T, K = y.shape
    D = w.shape[1]
    tm, tn = _tile(seq, tm), _tile(D, tn)
    rows_per_b = seq // tm
    vmem = 2 * tm * K * 2 + 2 * K * tn * 2 + 4 * tm * tn * 4
    return pl.pallas_call(
        _proj_residual_kernel,
        grid=(T // tm, D // tn),
        in_specs=[
            pl.BlockSpec((tm, K), lambda i, j: (i, 0)),
            pl.BlockSpec((K, tn), lambda i, j: (0, j)),
            pl.BlockSpec((tm, tn), lambda i, j: (i, j)),
            pl.BlockSpec((None, 1, tn), lambda i, j: (i // rows_per_b, 0, j)),
        ],
        out_specs=pl.BlockSpec((tm, tn), lambda i, j: (i, j)),
        out_shape=jax.ShapeDtypeStruct((T, D), F32),
        compiler_params=_params(("arbitrary", "arbitrary"), vmem),
        name="proj_residual",
    )(y, w, h, gate)


def _rotary(x, cos, sin):
    half = x.shape[-1] // 2
    x1, x2 = x[:, :half], x[:, half:]
    return jnp.concatenate([x1 * cos - x2 * sin, x1 * sin + x2 * cos], axis=-1)


def _retention_kernel(lg_ref, q_ref, k_ref, v_ref, g_ref, cos_ref, sin_ref, o_ref, state_ref,
                      *, chunk, n_chunks):
    @pl.when(pl.program_id(2) == 0)
    def _():
        state_ref[...] = jnp.zeros_like(state_ref)

    lg = lg_ref[pl.program_id(1)]
    row = lax.broadcasted_iota(jnp.int32, (chunk, chunk), 0)
    col = lax.broadcasted_iota(jnp.int32, (chunk, chunk), 1)
    dist = (row - col).astype(F32)
    d_local = jnp.where(dist >= 0, jnp.exp(lg * jnp.maximum(dist, 0.0)), 0.0)
    t = lax.broadcasted_iota(jnp.int32, (chunk, 1), 0).astype(F32)
    xi = jnp.exp(lg * (t + 1.0))
    zeta = jnp.exp(lg * (chunk - 1.0 - t))
    gamma_c = jnp.exp(jnp.full((1, 1), lg * chunk, F32))

    def body(ci, carry):
        r0 = pl.multiple_of(ci * chunk, chunk)
        cos = cos_ref[pl.ds(r0, chunk), :]
        sin = sin_ref[pl.ds(r0, chunk), :]
        q = _rotary(q_ref[pl.ds(r0, chunk), :].astype(F32), cos, sin).astype(BF16)
        k = (_rotary(k_ref[pl.ds(r0, chunk), :].astype(F32), cos, sin) * (RET_DK ** -0.5)).astype(BF16)
        v = v_ref[pl.ds(r0, chunk), :]
        scores = lax.dot_general(q, k, (((1,), (1,)), ((), ())), preferred_element_type=F32) * d_local
        inner = jnp.dot(scores.astype(BF16), v, preferred_element_type=F32)
        state = state_ref[...]
        cross = jnp.dot(q, state.astype(BF16), preferred_element_type=F32) * xi
        vz = (v.astype(F32) * zeta).astype(BF16)
        state_ref[...] = gamma_c * state + lax.dot_general(
            k, vz, (((0,), (0,)), ((), ())), preferred_element_type=F32)
        o = inner + cross
        ms = jnp.mean(o * o, axis=-1, keepdims=True)
        g = g_ref[pl.ds(r0, chunk), :].astype(F32)
        o_ref[pl.ds(r0, chunk), :] = (g * jax.nn.sigmoid(g) * (o * lax.rsqrt(ms + EPS))).astype(o_ref.dtype)
        return carry

    lax.fori_loop(0, n_chunks, body, 0, unroll=True)


def _retention(proj, rope, batch, seq, heads, *, chunk=256, rows=1024):
    T = proj.shape[0]
    chunk = _tile(seq, chunk)
    rows = _tile(seq, rows)
    nblk = seq // rows
    log_gamma = jnp.log1p(-jnp.exp2(-5.0 - jnp.arange(heads, dtype=F32)))
    kern = functools.partial(_retention_kernel, chunk=chunk, n_chunks=rows // chunk)
    qk_blocks = heads
    v_off = 2 * heads * RET_DK // RET_DV
    return pl.pallas_call(
        kern,
        grid=(batch, heads, nblk),
        in_specs=[
            pl.BlockSpec(memory_space=pltpu.SMEM),
            pl.BlockSpec((rows, RET_DK), lambda b, h, c: (b * nblk + c, h)),
            pl.BlockSpec((rows, RET_DK), lambda b, h, c: (b * nblk + c, qk_blocks + h)),
            pl.BlockSpec((rows, RET_DV), lambda b, h, c: (b * nblk + c, v_off + h)),
            pl.BlockSpec((rows, RET_DV), lambda b, h, c: (b * nblk + c, v_off + heads + h)),
            pl.BlockSpec((rows, RET_DK // 2), lambda b, h, c: (c, 0)),
            pl.BlockSpec((rows, RET_DK // 2), lambda b, h, c: (c, 0)),
        ],
        out_specs=pl.BlockSpec((rows, RET_DV), lambda b, h, c: (b * nblk + c, h)),
        out_shape=jax.ShapeDtypeStruct((T, heads * RET_DV), BF16),
        scratch_shapes=[pltpu.VMEM((RET_DK, RET_DV), F32)],
        compiler_params=_params(("arbitrary", "arbitrary", "arbitrary"), 16 * 1024 * 1024),
        name="retention",
    )(log_gamma, proj, proj, proj, proj, *rope)


def _bf16_pieces(x, n):
    pieces = []
    for _ in range(n):
        p = x.astype(BF16)
        pieces.append(p)
        x = x - p.astype(F32)
    return pieces


def _small_proj(h_ref, gain_ref, scale_ref, shift_ref, w_ref):
    u = _norm_mod(h_ref[...], gain_ref[...], scale_ref[...], shift_ref[...])
    u_hi, u_lo = _bf16_pieces(u, 2)
    w_hi, w_lo = _bf16_pieces(w_ref[...], 2)
    out = (jnp.dot(u_hi, w_hi, preferred_element_type=F32) + jnp.dot(u_hi, w_lo, preferred_element_type=F32)
           + jnp.dot(u_lo, w_hi, preferred_element_type=F32))
    return u, out


def _forget_kernel(h_ref, gain_ref, scale_ref, shift_ref, w_ref, bias_ref, o_ref, carry_ref,
                   *, tiles_per_seq, n_heads):
    @pl.when(pl.program_id(0) % tiles_per_seq == 0)
    def _():
        carry_ref[...] = jnp.zeros_like(carry_ref)

    _, f = _small_proj(h_ref, gain_ref, scale_ref, shift_ref, w_ref)
    log_f = jax.nn.log_sigmoid(f + bias_ref[...])
    tm = log_f.shape[0]
    row = lax.broadcasted_iota(jnp.int32, (tm, tm), 0)
    col = lax.broadcasted_iota(jnp.int32, (tm, tm), 1)
    tri = jnp.where(col <= row, 1.0, 0.0).astype(BF16)
    cum = sum(jnp.dot(tri, piece, preferred_element_type=F32) for piece in _bf16_pieces(log_f, 3))
    cum = cum + carry_ref[...]
    carry_ref[...] = cum[tm - 1:tm, :]
    neg = cum * (-LOG2E)
    lane = lax.broadcasted_iota(jnp.int32, (tm, FOX_DH), 1)
    for hh in range(n_heads):
        b = neg[:, hh:hh + 1]
        hi = b.astype(BF16).astype(F32)
        mid = (b - hi).astype(BF16).astype(F32)
        lo = b - hi - mid
        blk = jnp.where(lane == 0, hi, jnp.where(lane == 1, mid, jnp.where(lane == 2, lo, 0.0)))
        o_ref[:, hh * FOX_DH:(hh + 1) * FOX_DH] = blk.astype(o_ref.dtype)


def _forget_cumsum(h, gain, scale, shift, w_f, bias, seq, *, tm=512):
    T, D = h.shape
    nh = w_f.shape[1]
    tm = _tile(seq, tm)
    rows_per_b = seq // tm
    w_pad = jnp.zeros((D, LANES), F32).at[:, :nh].set(w_f)
    b_pad = jnp.zeros((1, LANES), F32).at[0, :nh].set(bias)
    kern = functools.partial(_forget_kernel, tiles_per_seq=rows_per_b, n_heads=nh)
    return pl.pallas_call(
        kern,
        grid=(T // tm,),
        in_specs=[
            pl.BlockSpec((tm, D), lambda i: (i, 0)),
            pl.BlockSpec((1, D), lambda i: (0, 0)),
            pl.BlockSpec((None, 1, D), lambda i: (i // rows_per_b, 0, 0)),
            pl.BlockSpec((None, 1, D), lambda i: (i // rows_per_b, 0, 0)),
            pl.BlockSpec((D, LANES), lambda i: (0, 0)),
            pl.BlockSpec((1, LANES), lambda i: (0, 0)),
        ],
        out_specs=pl.BlockSpec((tm, nh * FOX_DH), lambda i: (i, 0)),
        out_shape=jax.ShapeDtypeStruct((T, nh * FOX_DH), BF16),
        scratch_shapes=[pltpu.VMEM((1, LANES), F32)],
        compiler_params=_params(("arbitrary",), 2 * tm * D * 4 + 4 * tm * D * 4),
        name="forget_cumsum",
    )(h, gain.reshape(1, D), scale, shift, w_pad, b_pad)


def _router_kernel(h_ref, gain_ref, scale_ref, shift_ref, w_ref, u_ref, route_ref, w0_ref, w1_ref,
                   count_ref, carry_ref, *, n_experts):
    @pl.when(pl.program_id(0) == 0)
    def _():
        carry_ref[...] = jnp.zeros_like(carry_ref)

    u, logits = _small_proj(h_ref, gain_ref, scale_ref, shift_ref, w_ref)
    u_ref[...] = u.reshape(u_ref.shape).astype(u_ref.dtype)
    tm = logits.shape[0]
    lane = lax.broadcasted_iota(jnp.int32, (tm, LANES), 1)
    neg = -jnp.inf
    l0 = jnp.where(lane < n_experts, logits, neg)
    m0 = jnp.max(l0, axis=-1, keepdims=True)
    i0 = jnp.min(jnp.where(l0 == m0, lane, LANES), axis=-1, keepdims=True)
    l1 = jnp.where(lane == i0, neg, l0)
    m1 = jnp.max(l1, axis=-1, keepdims=True)
    i1 = jnp.min(jnp.where(l1 == m1, lane, LANES), axis=-1, keepdims=True)
    e = jnp.exp(m1 - m0)
    wgt0 = 1.0 / (1.0 + e)
    wgt1 = e / (1.0 + e)
    onehot = jnp.where(lane == i0, 1.0, jnp.where(lane == i1, 1.0, 0.0))
    row = lax.broadcasted_iota(jnp.int32, (tm, tm), 0)
    col = lax.broadcasted_iota(jnp.int32, (tm, tm), 1)
    tri = jnp.where(col < row, 1.0, 0.0).astype(BF16)
    before = jnp.dot(tri, onehot.astype(BF16), preferred_element_type=F32) + carry_ref[...]
    rank0 = jnp.sum(jnp.where(lane == i0, before, 0.0), axis=-1, keepdims=True)
    rank1 = jnp.sum(jnp.where(lane == i1, before, 0.0), axis=-1, keepdims=True)
    total = carry_ref[...] + jnp.sum(onehot, axis=0, keepdims=True)
    carry_ref[...] = total
    count_ref[...] = jnp.broadcast_to(total, count_ref.shape)
    route = jnp.where(lane == 0, i0.astype(F32),
                      jnp.where(lane == 1, i1.astype(F32),
                                jnp.where(lane == 2, rank0, jnp.where(lane == 3, rank1, 0.0))))
    route_ref[...] = route
    w0_ref[...] = jnp.broadcast_to(wgt0, w0_ref.shape)
    w1_ref[...] = jnp.broadcast_to(wgt1, w1_ref.shape)


def _router(h, gain, scale, shift, w_router, seq, *, tm=512):
    T, D = h.shape
    E = w_router.shape[1]
    tm = _tile(seq, tm)
    rows_per_b = seq // tm
    w_pad = jnp.zeros((D, LANES), F32).at[:, :E].set(w_router)
    kern = functools.partial(_router_kernel, n_experts=E)
    row_spec = pl.BlockSpec((tm, LANES), lambda i: (i, 0))
    row_shape = jax.ShapeDtypeStruct((T, LANES), F32)
    return pl.pallas_call(
        kern,
        grid=(T // tm,),
        in_specs=[
            pl.BlockSpec((tm, D), lambda i: (i, 0)),
            pl.BlockSpec((1, D), lambda i: (0, 0)),
            pl.BlockSpec((None, 1, D), lambda i: (i // rows_per_b, 0, 0)),
            pl.BlockSpec((None, 1, D), lambda i: (i // rows_per_b, 0, 0)),
            pl.BlockSpec((D, LANES), lambda i: (0, 0)),
        ],
        out_specs=[
            pl.BlockSpec((tm,) + _row_tile(D), lambda i: (i, 0, 0)),
            row_spec, row_spec, row_spec,
            pl.BlockSpec((8, LANES), lambda i: (0, 0)),
        ],
        out_shape=[
            jax.ShapeDtypeStruct((T,) + _row_tile(D), BF16),
            row_shape, row_shape, row_shape,
            jax.ShapeDtypeStruct((8, LANES), F32),
        ],
        scratch_shapes=[pltpu.VMEM((1, LANES), F32)],
        compiler_params=_params(("arbitrary",), 2 * tm * D * 4 + 2 * tm * D * 2 + 4 * tm * D * 4),
        name="moe_router",
    )(h, gain.reshape(1, D), scale, shift, w_pad)


def _fox_kernel(ti_ref, tj_ref, q_ref, k_ref, e_ref, v_ref, o_ref, kp_ref, vp_ref, m_ref, acc_ref,
                *bufs, tq, n_off, n_diag):
    lane = lax.broadcasted_iota(jnp.int32, k_ref.shape, 1)
    kp_ref[:FOX_DH, :] = k_ref[...].T
    kp_ref[FOX_DH:, :] = e_ref[...].T
    vp_ref[:, :FOX_DH] = v_ref[...]
    vp_ref[:, FOX_DH:] = jnp.where(lane == 0, 1.0, 0.0).astype(BF16)
    m_ref[...] = jnp.full(m_ref.shape, -jnp.inf, F32)
    acc_ref[...] = jnp.zeros_like(acc_ref)

    LAG = FOX_PIPELINE_LAG
    nbuf = 2 * LAG
    lane = lax.broadcasted_iota(jnp.int32, (tq, FOX_DH), 1)
    ones = jnp.where(lane < 3, 1.0, 0.0).astype(BF16)
    s_refs, p_refs, a_refs = bufs[:nbuf], bufs[nbuf:2 * nbuf], bufs[2 * nbuf:]

    def tile_rows(idx_ref, t):
        return pl.ds(pl.multiple_of(idx_ref[t] * tq, tq), tq)

    def logits(t, par):
        qp = jnp.concatenate([q_ref[tile_rows(ti_ref, t), :], ones], axis=-1)
        kp = kp_ref[:, tile_rows(tj_ref, t)]
        s_refs[par][...] = jnp.dot(qp, kp, preferred_element_type=F32)

    def softmax(t, par, diag):
        rows = tile_rows(ti_ref, t)
        s = s_refs[par][...]
        if diag:
            row = lax.broadcasted_iota(jnp.int32, (tq, tq), 0)
            col = lax.broadcasted_iota(jnp.int32, (tq, tq), 1)
            s = jnp.where(col <= row, s, -jnp.inf)
        m = m_ref[rows, :]
        m_new = jnp.maximum(m, jnp.max(s, axis=-1, keepdims=True))
        m_ref[rows, :] = m_new
        a_refs[par][...] = jnp.exp2(m - m_new)
        p_refs[par][...] = jnp.exp2(s - m_new).astype(BF16)

    def values(t, par, diag):
        rows = tile_rows(ti_ref, t)
        vp = vp_ref[tile_rows(tj_ref, t), :]
        acc = a_refs[par][...] * acc_ref[rows, :] + jnp.dot(p_refs[par][...], vp,
                                                            preferred_element_type=F32)
        if diag:
            o_ref[rows, :] = (acc[:, :FOX_DH] / acc[:, FOX_DH:FOX_DH + 1]).astype(o_ref.dtype)
        else:
            acc_ref[rows, :] = acc

    n = n_off + n_diag
    last = n - 1 + 2 * LAG

    def plan(u):
        tv, ts = u - 2 * LAG, u - LAG
        return (0 <= tv < n, tv >= n_off), (0 <= ts < n, ts >= n_off), u < n

    def step(u, slot, stages):
        (do_v, diag_v), (do_s, diag_s), do_l = stages
        if do_v:
            values(u - 2 * LAG, slot, diag_v)
        if do_s:
            softmax(u - LAG, (slot + LAG) % nbuf, diag_s)
        if do_l:
            logits(u, slot)

    def uniform(u0):
        plans = {plan(u0 + g) for g in range(nbuf)}
        stages = next(iter(plans))
        full = stages[0][0] and stages[1][0] and stages[2]
        return stages if len(plans) == 1 and full and u0 + nbuf - 1 <= last else None

    u = 0
    while u <= last:
        stages = uniform(u) if u % nbuf == 0 else None
        if stages is None:
            step(u, u % nbuf, plan(u))
            u += 1
            continue
        count = 1
        while uniform(u + count * nbuf) == stages:
            count += 1

        def group(d, c, base=u, stages=stages):
            for g in range(nbuf):
                step(base + nbuf * d + g, g, stages)
            return c

        lax.fori_loop(0, count, group, 0)
        u += count * nbuf


def _fox_attention(q, kv, e, batch, seq, heads, *, tq=512):
    T = q.shape[0]
    tq = _tile(seq, tq)
    nq = seq // tq
    below = [(i, j) for i in range(nq) for j in range(i)]
    tiles = below + [(i, i) for i in range(nq)]
    ti = jnp.array([t[0] for t in tiles], jnp.int32)
    tj = jnp.array([t[1] for t in tiles], jnp.int32)
    kern = functools.partial(_fox_kernel, tq=tq, n_off=len(below), n_diag=nq)
    nbuf = 2 * FOX_PIPELINE_LAG
    grid_spec = pltpu.PrefetchScalarGridSpec(
        num_scalar_prefetch=2,
        grid=(batch, heads),
        in_specs=[
            pl.BlockSpec((seq, FOX_DH), lambda b, h, ti, tj: (b, h)),
            pl.BlockSpec((seq, FOX_DH), lambda b, h, ti, tj: (b, h)),
            pl.BlockSpec((seq, FOX_DH), lambda b, h, ti, tj: (b, h)),
            pl.BlockSpec((seq, FOX_DH), lambda b, h, ti, tj: (b, heads + h)),
        ],
        out_specs=pl.BlockSpec((seq, FOX_DH), lambda b, h, ti, tj: (b, h)),
        scratch_shapes=([pltpu.VMEM((2 * FOX_DH, seq), BF16), pltpu.VMEM((seq, 2 * FOX_DH), BF16),
                         pltpu.VMEM((seq, 1), F32), pltpu.VMEM((seq, 2 * FOX_DH), F32)]
                        + [pltpu.VMEM((tq, tq), F32)] * nbuf + [pltpu.VMEM((tq, tq), BF16)] * nbuf
                        + [pltpu.VMEM((tq, 1), F32)] * nbuf),
    )
    vmem = 14 * seq * FOX_DH * 2 + seq * (LANES + 2 * FOX_DH) * 4 + nbuf * tq * (6 * tq + 4 * LANES)
    return pl.pallas_call(
        kern,
        grid_spec=grid_spec,
        out_shape=jax.ShapeDtypeStruct((T, heads * FOX_DH), BF16),
        compiler_params=_params(("arbitrary", "arbitrary"), vmem),
        name="fox_attention",
    )(ti, tj, q, kv, e, kv)


def _swiglu_step(u, w1_ref, w3_ref, w2_ref, acc_ref):
    a = jnp.dot(u, w1_ref[...].astype(BF16), preferred_element_type=F32)
    g = jnp.dot(u, w3_ref[...].astype(BF16), preferred_element_type=F32)
    mid = (a * jax.nn.sigmoid(a) * g).astype(BF16)
    acc_ref[...] += jnp.dot(mid, w2_ref[...].astype(BF16), preferred_element_type=F32)


def _dense_ffn_kernel(h_ref, gain_ref, scale_ref, shift_ref, gate_ref, w1_ref, w3_ref, w2_ref,
                      o_ref, u_ref, acc_ref):
    f = pl.program_id(1)

    @pl.when(f == 0)
    def _():
        u_ref[...] = _norm_mod(h_ref[...], gain_ref[...], scale_ref[...], shift_ref[...]).astype(BF16)
        acc_ref[...] = jnp.zeros_like(acc_ref)

    _swiglu_step(u_ref[...], w1_ref, w3_ref, w2_ref, acc_ref)

    @pl.when(f == pl.num_programs(1) - 1)
    def _():
        o_ref[...] = h_ref[...] + gate_ref[...] * acc_ref[...]


def _dense_ffn(h, gain, scale, shift, gate, w13, w2, seq, *, tm=512, tf=512):
    T, D = h.shape
    F = w2.shape[0]
    tm, tf = _tile(seq, tm), _tile(F, tf)
    nf = F // tf
    rows_per_b = seq // tm
    vec = pl.BlockSpec((None, 1, D), lambda i, f: (i // rows_per_b, 0, 0))
    vmem = 4 * tm * D * 4 + tm * D * 2 + tm * D * 4 + 2 * 3 * D * tf * 2
    return pl.pallas_call(
        _dense_ffn_kernel,
        grid=(T // tm, nf),
        in_specs=[
            pl.BlockSpec((tm, D), lambda i, f: (i, 0)),
            pl.BlockSpec((1, D), lambda i, f: (0, 0)),
            vec, vec, vec,
            pl.BlockSpec((D, tf), lambda i, f: (0, f)),
            pl.BlockSpec((D, tf), lambda i, f: (0, nf + f)),
            pl.BlockSpec((tf, D), lambda i, f: (f, 0)),
        ],
        out_specs=pl.BlockSpec((tm, D), lambda i, f: (i, 0)),
        out_shape=jax.ShapeDtypeStruct((T, D), F32),
        scratch_shapes=[pltpu.VMEM((tm, D), BF16), pltpu.VMEM((tm, D), F32)],
        compiler_params=_params(("arbitrary", "arbitrary"), vmem),
        name="dense_swiglu",
    )(h, gain.reshape(1, D), scale, shift, gate, w13, w13, w2)


def _expert_ffn_kernel(te_ref, tr_ref, x_ref, w1_ref, w3_ref, w2_ref, o_ref, x2_ref, acc_ref):
    i, f = pl.program_id(0), pl.program_id(1)
    rows = tr_ref[i]
    quarter = x2_ref.shape[0] // 4

    @pl.when(f == 0)
    def _():
        x2_ref[...] = x_ref[...].reshape(x2_ref.shape)
        acc_ref[...] = jnp.zeros_like(acc_ref)

    for q in range(1, 5):
        @pl.when(jnp.logical_and(rows > (q - 1) * quarter, rows <= q * quarter))
        def _(n=q * quarter):
            _swiglu_step(x2_ref[:n, :], w1_ref, w3_ref, w2_ref, acc_ref.at[pl.ds(0, n), :])

    @pl.when(f == pl.num_programs(1) - 1)
    def _():
        o_ref[...] = acc_ref[...].astype(o_ref.dtype).reshape(o_ref.shape)


def _expert_ffn(xs, tile_expert, tile_rows, w13, w2, layer, *, tm, tf=256):
    P = xs.shape[0]
    row_tile = xs.shape[1:]
    _, E, F, D = w2.shape
    tf = _tile(F, tf)
    nf = F // tf

    def fidx(i, f, nu):
        return jnp.where(nu[i] > 0, f, nf - 1)

    grid_spec = pltpu.PrefetchScalarGridSpec(
        num_scalar_prefetch=2,
        grid=(P // tm, nf),
        in_specs=[
            pl.BlockSpec((tm,) + row_tile, lambda i, f, te, nu: (i, 0, 0)),
            pl.BlockSpec((None, None, D, tf), lambda i, f, te, nu: (layer, te[i], 0, fidx(i, f, nu))),
            pl.BlockSpec((None, None, D, tf), lambda i, f, te, nu: (layer, te[i], 0, nf + fidx(i, f, nu))),
            pl.BlockSpec((None, None, tf, D), lambda i, f, te, nu: (layer, te[i], fidx(i, f, nu), 0)),
        ],
        out_specs=pl.BlockSpec((tm,) + row_tile, lambda i, f, te, nu: (i, 0, 0)),
        scratch_shapes=[pltpu.VMEM((tm, D), BF16), pltpu.VMEM((tm, D), F32)],
    )
    w_bytes = jnp.dtype(w2.dtype).itemsize
    vmem = 5 * tm * D * 2 + tm * D * 4 + 3 * D * tf * (2 * w_bytes + 2) + 4 * tm * tf * 4
    return pl.pallas_call(
        _expert_ffn_kernel,
        grid_spec=grid_spec,
        out_shape=jax.ShapeDtypeStruct((P,) + row_tile, BF16),
        compiler_params=_params(("arbitrary", "arbitrary"), vmem),
        name="expert_swiglu",
    )(tile_expert, tile_rows, xs, w13, w13, w2)


def _dispatch_kernel(pos_ref, u_ref, xs_in_ref, xs_ref, sem, *, chunk):
    del xs_in_ref

    def copy(t, k):
        return pltpu.make_async_copy(u_ref.at[t], xs_ref.at[pos_ref[TOP_K * t + k]], sem)

    def start(t, c):
        for k in range(TOP_K):
            copy(t, k).start()
        return c

    def wait(t, c):
        for k in range(TOP_K):
            copy(t, k).wait()
        return c

    lax.fori_loop(0, chunk, start, 0, unroll=4)
    lax.fori_loop(0, chunk, wait, 0, unroll=4)


def _dispatch(u3, pos_flat, n_slots, *, chunk=512):
    T = u3.shape[0]
    chunk = _tile(T, chunk)
    xs0 = jnp.zeros((n_slots,) + u3.shape[1:], u3.dtype)
    kern = functools.partial(_dispatch_kernel, chunk=chunk)
    return pl.pallas_call(
        kern,
        grid=(T // chunk,),
        in_specs=[
            pl.BlockSpec((TOP_K * chunk,), lambda i: (i,), memory_space=pltpu.SMEM),
            pl.BlockSpec((chunk,) + u3.shape[1:], lambda i: (i, 0, 0)),
            pl.BlockSpec(memory_space=pl.ANY),
        ],
        out_specs=pl.BlockSpec(memory_space=pl.ANY),
        out_shape=jax.ShapeDtypeStruct(xs0.shape, xs0.dtype),
        scratch_shapes=[pltpu.SemaphoreType.DMA(())],
        input_output_aliases={2: 0},
        compiler_params=_params(("arbitrary",), 4 * chunk * u3.shape[1] * u3.shape[2] * 2),
        name="moe_dispatch",
    )(pos_flat, u3, xs0)


def _combine_kernel(pos_ref, ys_ref, w0_ref, w1_ref, h_ref, gate_ref, o_ref, buf_ref, sem, *, chunk):
    def copy(t, k):
        return pltpu.make_async_copy(ys_ref.at[pos_ref[TOP_K * t + k]], buf_ref.at[k, t], sem)

    def start(t, c):
        for k in range(TOP_K):
            copy(t, k).start()
        return c

    def wait(t, c):
        for k in range(TOP_K):
            copy(t, k).wait()
        return c

    lax.fori_loop(0, chunk, start, 0, unroll=4)
    lax.fori_loop(0, chunk, wait, 0, unroll=4)
    y = w0_ref[...] * buf_ref[0].astype(F32) + w1_ref[...] * buf_ref[1].astype(F32)
    o_ref[...] = h_ref[...] + gate_ref[...] * y.reshape(o_ref.shape)


def _combine(ys3, pos_flat, w0, w1, h, gate, seq, *, chunk=512):
    T, D = h.shape
    tile = ys3.shape[1:]
    chunk = _tile(seq, chunk)
    rows_per_b = seq // chunk
    kern = functools.partial(_combine_kernel, chunk=chunk)
    wspec = pl.BlockSpec((chunk, 1, LANES), lambda i: (i, 0, 0))
    return pl.pallas_call(
        kern,
        grid=(T // chunk,),
        in_specs=[
            pl.BlockSpec((TOP_K * chunk,), lambda i: (i,), memory_space=pltpu.SMEM),
            pl.BlockSpec(memory_space=pl.ANY),
            wspec, wspec,
            pl.BlockSpec((chunk, D), lambda i: (i, 0)),
            pl.BlockSpec((None, 1, D), lambda i: (i // rows_per_b, 0, 0)),
        ],
        out_specs=pl.BlockSpec((chunk, D), lambda i: (i, 0)),
        out_shape=jax.ShapeDtypeStruct((T, D), F32),
        scratch_shapes=[pltpu.VMEM((TOP_K, chunk) + tile, BF16), pltpu.SemaphoreType.DMA(())],
        compiler_params=_params(("arbitrary",), 10 * chunk * D * 4),
        name="moe_combine",
    )(pos_flat, ys3, w0.reshape(T, 1, LANES), w1.reshape(T, 1, LANES), h, gate)


def _moe_layer(h, gain, scale, shift, gate, w_router, w13, w2, layer, seq, *, tm=1024):
    T, D = h.shape
    E = w_router.shape[1]
    u3, route, w0, w1, counts = _router(h, gain, scale, shift, w_router, seq)

    expert = route[:, :TOP_K].astype(jnp.int32)
    rank = route[:, TOP_K:2 * TOP_K].astype(jnp.int32)
    count = counts[0, :E].astype(jnp.int32)
    padded = ((count + tm - 1) // tm) * tm
    ends = jnp.cumsum(padded)
    starts = ends - padded
    pos_flat = (starts[expert] + rank).reshape(-1)
    n_tiles = (TOP_K * T) // tm + E
    tile_start = jnp.arange(n_tiles, dtype=jnp.int32) * tm
    last_used = jnp.maximum(ends[-1] // tm - 1, 0)
    tile_expert = jnp.minimum(jnp.searchsorted(ends, tile_start, side="right"), E - 1).astype(jnp.int32)
    tile_rows = jnp.clip(starts[tile_expert] + count[tile_expert] - tile_start, 0, tm).astype(jnp.int32)
    tile_rows = jnp.where(tile_start < ends[-1], tile_rows, 0)
    tile_expert = jnp.where(tile_start < ends[-1], tile_expert, tile_expert[last_used])

    xs3 = _dispatch(u3, pos_flat, n_tiles * tm)
    ys3 = _expert_ffn(xs3, tile_expert, tile_rows, w13, w2, layer, tm=tm)
    return _combine(ys3, pos_flat, w0, w1, h, gate, seq)


def kernel(x, c, norm_gain, mod_w, mod_b, ret_w_in, ret_w_out, fox_w_q, fox_q_gain, fox_w_out,
           kv_norm_gain, kv_mod_w, kv_mod_b, kv_w, kv_forget_bias, kv_k_gain,
           ffn_w13, ffn_w2, moe_router, moe_w13, moe_w2):
    B, S, D = x.shape
    T = B * S
    depth = mod_w.shape[0]
    n_ret = ret_w_in.shape[0]
    ret_heads = ret_w_out.shape[1] // RET_DV
    fox_heads = fox_w_q.shape[2] // FOX_DH
    fox_width = fox_heads * FOX_DH

    c8 = jnp.zeros((8, D), F32).at[:B].set(c)
    mod = _mod_call(c8, mod_w, mod_b)[:, :B]
    kv_mod = _mod_call(c8, kv_mod_w[None], kv_mod_b[None])[0, :B]

    def vecs(m, n):
        return [v.reshape(B, 1, D) for v in jnp.split(m, n, axis=-1)]

    half = RET_DK // 2
    inv = 1.0 / (ROPE_BASE ** (jnp.arange(half, dtype=F32) / half))
    ang = jnp.arange(S, dtype=F32)[:, None] * inv[None, :]
    rope = (jnp.cos(ang), jnp.sin(ang))

    h = x.reshape(T, D)
    kv_sh = decay_sh = None
    for i in range(depth):
        sh1, sc1, g1, sh2, sc2, g2 = vecs(mod[i], 6)
        if i < n_ret:
            proj = _nm_matmul(h, norm_gain[i, 0], sc1, sh1, ret_w_in[i].astype(BF16), S)
            y = _retention(proj, rope, B, S, ret_heads)
            h = _proj_residual(y, ret_w_out[i].astype(BF16), h, g1, S)
        else:
            j = i - n_ret
            q = _nm_matmul(h, norm_gain[i, 0], sc1, sh1, fox_w_q[j].astype(BF16), S,
                           head_gain=fox_q_gain[j], special_cols=fox_width,
                           head_scale=LOG2E * FOX_DH ** -0.5)
            y = _fox_attention(q, kv_sh, decay_sh, B, S, fox_heads)
            h = _proj_residual(y, fox_w_out[j].astype(BF16), h, g1, S, tn=1024)
        if i % 2 == 0:
            h = _dense_ffn(h, norm_gain[i, 1], sc2, sh2, g2,
                           ffn_w13[i // 2].astype(BF16), ffn_w2[i // 2].astype(BF16), S)
        else:
            h = _moe_layer(h, norm_gain[i, 1], sc2, sh2, g2, moe_router[i // 2], moe_w13, moe_w2, i // 2, S)
        if i == n_ret - 1:
            ksh, ksc = vecs(kv_mod, 2)
            kv_sh = _nm_matmul(h, kv_norm_gain, ksc, ksh, kv_w[:, :2 * fox_width].astype(BF16), S,
                               head_gain=kv_k_gain, special_cols=fox_width)
            decay_sh = _forget_cumsum(h, kv_norm_gain, ksc, ksh, kv_w[:, 2 * fox_width:], kv_forget_bias, S)
    return h.reshape(B, S, D)
```

```python
import functools
import math

import jax
import jax.numpy as jnp
from jax import lax
from jax.experimental import pallas as pl
from jax.experimental.pallas import tpu as pltpu

F32 = jnp.float32
BF16 = jnp.bfloat16
HIGHEST = lax.Precision.HIGHEST

EPS = 1e-6
ROPE_BASE = 10000.0
RET_DK = 256
RET_DV = 2 * RET_DK
FOX_DH = 128
FOX_PIPELINE_LAG = 2
TOP_K = 2
LOG2E = math.log2(math.e)
LANES = 128
SUBLANES_BF16 = 16
VMEM_LIMIT_CAP = 56 * 1024 * 1024


def _params(semantics, vmem_bytes):
    limit = int(min(max(2 * vmem_bytes, 16 * 1024 * 1024), VMEM_LIMIT_CAP))
    return pltpu.CompilerParams(dimension_semantics=semantics, vmem_limit_bytes=limit)


def _tile(n, pref):
    t = min(n, pref)
    assert n % t == 0, (n, pref)
    return t


def _row_tile(d):
    assert d % (SUBLANES_BF16 * LANES) == 0, d
    return (d // LANES, LANES)


def _norm_mod(h, gain, scale, shift):
    ms = jnp.mean(h * h, axis=-1, keepdims=True)
    return h * lax.rsqrt(ms + EPS) * (gain * (1.0 + scale)) + shift


def _mod_kernel(c_ref, w_ref, b_ref, o_ref):
    c = c_ref[...]
    cond = c * jax.nn.sigmoid(c)
    o_ref[...] = jnp.dot(cond, w_ref[...], precision=HIGHEST, preferred_element_type=F32) + b_ref[...]


def _mod_call(c8, w, b):
    L, D, N = w.shape
    tn = _tile(N, 1024)
    return pl.pallas_call(
        _mod_kernel,
        grid=(L, N // tn),
        in_specs=[
            pl.BlockSpec((8, D), lambda l, j: (0, 0)),
            pl.BlockSpec((None, D, tn), lambda l, j: (l, 0, j)),
            pl.BlockSpec((None, 1, tn), lambda l, j: (l, 0, j)),
        ],
        out_specs=pl.BlockSpec((None, 8, tn), lambda l, j: (l, 0, j)),
        out_shape=jax.ShapeDtypeStruct((L, 8, N), F32),
        compiler_params=_params(("arbitrary", "arbitrary"), 2 * D * tn * 4),
        name="adaln_mod",
    )(c8, w, b.reshape(L, 1, N))


def _nm_matmul_kernel(h_ref, gain_ref, scale_ref, shift_ref, w_ref, hg_ref, o_ref, u_ref,
                      *, n_special, head_scale):
    j = pl.program_id(1)

    @pl.when(j == 0)
    def _():
        u_ref[...] = _norm_mod(h_ref[...], gain_ref[...], scale_ref[...], shift_ref[...]).astype(BF16)

    acc = jnp.dot(u_ref[...], w_ref[...], preferred_element_type=F32)

    if n_special == 0:
        o_ref[...] = acc.astype(o_ref.dtype)
        return

    @pl.when(j < n_special)
    def _():
        hg = hg_ref[...] * head_scale
        for c in range(acc.shape[1] // FOX_DH):
            blk = acc[:, c * FOX_DH:(c + 1) * FOX_DH]
            ms = jnp.mean(blk * blk, axis=-1, keepdims=True)
            o_ref[:, c * FOX_DH:(c + 1) * FOX_DH] = (blk * lax.rsqrt(ms + EPS) * hg).astype(o_ref.dtype)

    @pl.when(j >= n_special)
    def _():
        o_ref[...] = acc.astype(o_ref.dtype)


def _nm_matmul(h, gain, scale, shift, w, seq, *, special_cols=0, head_gain=None, head_scale=1.0,
               tm=1024, tn=1024):
    T, D = h.shape
    N = w.shape[1]
    tm, tn = _tile(seq, tm), _tile(N, tn)
    assert special_cols % tn == 0
    rows_per_b = seq // tm
    gain_row = (jnp.ones((FOX_DH,), F32) if head_gain is None else head_gain).reshape(1, FOX_DH)
    vmem = 2 * tm * D * 4 + 2 * D * tn * 2 + 2 * tm * tn * 2 + tm * D * 2
    kern = functools.partial(_nm_matmul_kernel, n_special=special_cols // tn, head_scale=head_scale)
    return pl.pallas_call(
        kern,
        grid=(T // tm, N // tn),
        in_specs=[
            pl.BlockSpec((tm, D), lambda i, j: (i, 0)),
            pl.BlockSpec((1, D), lambda i, j: (0, 0)),
            pl.BlockSpec((None, 1, D), lambda i, j: (i // rows_per_b, 0, 0)),
            pl.BlockSpec((None, 1, D), lambda i, j: (i // rows_per_b, 0, 0)),
            pl.BlockSpec((D, tn), lambda i, j: (0, j)),
            pl.BlockSpec((1, FOX_DH), lambda i, j: (0, 0)),
        ],
        out_specs=pl.BlockSpec((tm, tn), lambda i, j: (i, j)),
        out_shape=jax.ShapeDtypeStruct((T, N), BF16),
        scratch_shapes=[pltpu.VMEM((tm, D), BF16)],
        compiler_params=_params(("arbitrary", "arbitrary"), vmem),
        name="norm_mod_proj",
    )(h, gain.reshape(1, D), scale, shift, w, gain_row)


def _proj_residual_kernel(y_ref, w_ref, h_ref, gate_ref, o_ref):
    acc = jnp.dot(y_ref[...], w_ref[...], preferred_element_type=F32)
    o_ref[...] = h_ref[...] + gate_ref[...] * acc


def _proj_residual(y, w, h, gate, seq, *, tm=1024, tn=512):
    T, K = y.shape
    D = w.shape[1]
    tm, tn = _tile(seq, tm), _tile(D, tn)
    rows_per_b = seq // tm
    vmem = 2 * tm * K * 2 + 2 * K * tn * 2 + 4 * tm * tn * 4
    return pl.pallas_call(
        _proj_residual_kernel,
        grid=(T // tm, D // tn),
        in_specs=[
            pl.BlockSpec((tm, K), lambda i, j: (i, 0)),
            pl.BlockSpec((K, tn), lambda i, j: (0, j)),
            pl.BlockSpec((tm, tn), lambda i, j: (i, j)),
            pl.BlockSpec((None, 1, tn), lambda i, j: (i // rows_per_b, 0, j)),
        ],
        out_specs=pl.BlockSpec((tm, tn), lambda i, j: (i, j)),
        out_shape=jax.ShapeDtypeStruct((T, D), F32),
        compiler_params=_params(("arbitrary", "arbitrary"), vmem),
        name="proj_residual",
    )(y, w, h, gate)


def _rotary(x, cos, sin):
    half = x.shape[-1] // 2
    x1, x2 = x[:, :half], x[:, half:]
    return jnp.concatenate([x1 * cos - x2 * sin, x1 * sin + x2 * cos], axis=-1)


def _retention_kernel(lg_ref, q_ref, k_ref, v_ref, g_ref, cos_ref, sin_ref, o_ref, state_ref,
                      *, chunk, n_chunks):
    @pl.when(pl.program_id(2) == 0)
    def _():
        state_ref[...] = jnp.zeros_like(state_ref)

    lg = lg_ref[pl.program_id(1)]
    row = lax.broadcasted_iota(jnp.int32, (chunk, chunk), 0)
    col = lax.broadcasted_iota(jnp.int32, (chunk, chunk), 1)
    dist = (row - col).astype(F32)
    d_local = jnp.where(dist >= 0, jnp.exp(lg * jnp.maximum(dist, 0.0)), 0.0)
    t = lax.broadcasted_iota(jnp.int32, (chunk, 1), 0).astype(F32)
    xi = jnp.exp(lg * (t + 1.0))
    zeta = jnp.exp(lg * (chunk - 1.0 - t))
    gamma_c = jnp.exp(jnp.full((1, 1), lg * chunk, F32))

    def body(ci, carry):
        r0 = pl.multiple_of(ci * chunk, chunk)
        cos = cos_ref[pl.ds(r0, chunk), :]
        sin = sin_ref[pl.ds(r0, chunk), :]
        q = _rotary(q_ref[pl.ds(r0, chunk), :].astype(F32), cos, sin).astype(BF16)
        k = (_rotary(k_ref[pl.ds(r0, chunk), :].astype(F32), cos, sin) * (RET_DK ** -0.5)).astype(BF16)
        v = v_ref[pl.ds(r0, chunk), :]
        scores = lax.dot_general(q, k, (((1,), (1,)), ((), ())), preferred_element_type=F32) * d_local
        inner = jnp.dot(scores.astype(BF16), v, preferred_element_type=F32)
        state = state_ref[...]
        cross = jnp.dot(q, state.astype(BF16), preferred_element_type=F32) * xi
        vz = (v.astype(F32) * zeta).astype(BF16)
        state_ref[...] = gamma_c * state + lax.dot_general(
            k, vz, (((0,), (0,)), ((), ())), preferred_element_type=F32)
        o = inner + cross
        ms = jnp.mean(o * o, axis=-1, keepdims=True)
        g = g_ref[pl.ds(r0, chunk), :].astype(F32)
        o_ref[pl.ds(r0, chunk), :] = (g * jax.nn.sigmoid(g) * (o * lax.rsqrt(ms + EPS))).astype(o_ref.dtype)
        return carry

    lax.fori_loop(0, n_chunks, body, 0, unroll=True)


def _retention(proj, rope, batch, seq, heads, *, chunk=256, rows=1024):
    T = proj.shape[0]
    chunk = _tile(seq, chunk)
    rows = _tile(seq, rows)
    nblk = seq // rows
    log_gamma = jnp.log1p(-jnp.exp2(-5.0 - jnp.arange(heads, dtype=F32)))
    kern = functools.partial(_retention_kernel, chunk=chunk, n_chunks=rows // chunk)
    qk_blocks = heads
    v_off = 2 * heads * RET_DK // RET_DV
    return pl.pallas_call(
        kern,
        grid=(batch, heads, nblk),
        in_specs=[
            pl.BlockSpec(memory_space=pltpu.SMEM),
            pl.BlockSpec((rows, RET_DK), lambda b, h, c: (b * nblk + c, h)),
            pl.BlockSpec((rows, RET_DK), lambda b, h, c: (b * nblk + c, qk_blocks + h)),
            pl.BlockSpec((rows, RET_DV), lambda b, h, c: (b * nblk + c, v_off + h)),
            pl.BlockSpec((rows, RET_DV), lambda b, h, c: (b * nblk + c, v_off + heads + h)),
            pl.BlockSpec((rows, RET_DK // 2), lambda b, h, c: (c, 0)),
            pl.BlockSpec((rows, RET_DK // 2), lambda b, h, c: (c, 0)),
        ],
        out_specs=pl.BlockSpec((rows, RET_DV), lambda b, h, c: (b * nblk + c, h)),
        out_shape=jax.ShapeDtypeStruct((T, heads * RET_DV), BF16),
        scratch_shapes=[pltpu.VMEM((RET_DK, RET_DV), F32)],
        compiler_params=_params(("arbitrary", "arbitrary", "arbitrary"), 16 * 1024 * 1024),
        name="retention",
    )(log_gamma, proj, proj, proj, proj, *rope)


def _bf16_pieces(x, n):
    pieces = []
    for _ in range(n):
        p = x.astype(BF16)
        pieces.append(p)
        x = x - p.astype(F32)
    return pieces


def _small_proj(h_ref, gain_ref, scale_ref, shift_ref, w_ref):
    u = _norm_mod(h_ref[...], gain_ref[...], scale_ref[...], shift_ref[...])
    u_hi, u_lo = _bf16_pieces(u, 2)
    w_hi, w_lo = _bf16_pieces(w_ref[...], 2)
    out = (jnp.dot(u_hi, w_hi, preferred_element_type=F32) + jnp.dot(u_hi, w_lo, preferred_element_type=F32)
           + jnp.dot(u_lo, w_hi, preferred_element_type=F32))
    return u, out


def _forget_kernel(h_ref, gain_ref, scale_ref, shift_ref, w_ref, bias_ref, o_ref, carry_ref,
                   *, tiles_per_seq, n_heads):
    @pl.when(pl.program_id(0) % tiles_per_seq == 0)
    def _():
        carry_ref[...] = jnp.zeros_like(carry_ref)

    _, f = _small_proj(h_ref, gain_ref, scale_ref, shift_ref, w_ref)
    log_f = jax.nn.log_sigmoid(f + bias_ref[...])
    tm = log_f.shape[0]
    row = lax.broadcasted_iota(jnp.int32, (tm, tm), 0)
    col = lax.broadcasted_iota(jnp.int32, (tm, tm), 1)
    tri = jnp.where(col <= row, 1.0, 0.0).astype(BF16)
    cum = sum(jnp.dot(tri, piece, preferred_element_type=F32) for piece in _bf16_pieces(log_f, 3))
    cum = cum + carry_ref[...]
    carry_ref[...] = cum[tm - 1:tm, :]
    neg = cum * (-LOG2E)
    lane = lax.broadcasted_iota(jnp.int32, (tm, FOX_DH), 1)
    for hh in range(n_heads):
        b = neg[:, hh:hh + 1]
        hi = b.astype(BF16).astype(F32)
        mid = (b - hi).astype(BF16).astype(F32)
        lo = b - hi - mid
        blk = jnp.where(lane == 0, hi, jnp.where(lane == 1, mid, jnp.where(lane == 2, lo, 0.0)))
        o_ref[:, hh * FOX_DH:(hh + 1) * FOX_DH] = blk.astype(o_ref.dtype)


def _forget_cumsum(h, gain, scale, shift, w_f, bias, seq, *, tm=512):
    T, D = h.shape
    nh = w_f.shape[1]
    tm = _tile(seq, tm)
    rows_per_b = seq // tm
    w_pad = jnp.zeros((D, LANES), F32).at[:, :nh].set(w_f)
    b_pad = jnp.zeros((1, LANES), F32).at[0, :nh].set(bias)
    kern = functools.partial(_forget_kernel, tiles_per_seq=rows_per_b, n_heads=nh)
    return pl.pallas_call(
        kern,
        grid=(T // tm,),
        in_specs=[
            pl.BlockSpec((tm, D), lambda i: (i, 0)),
            pl.BlockSpec((1, D), lambda i: (0, 0)),
            pl.BlockSpec((None, 1, D), lambda i: (i // rows_per_b, 0, 0)),
            pl.BlockSpec((None, 1, D), lambda i: (i // rows_per_b, 0, 0)),
            pl.BlockSpec((D, LANES), lambda i: (0, 0)),
            pl.BlockSpec((1, LANES), lambda i: (0, 0)),
        ],
        out_specs=pl.BlockSpec((tm, nh * FOX_DH), lambda i: (i, 0)),
        out_shape=jax.ShapeDtypeStruct((T, nh * FOX_DH), BF16),
        scratch_shapes=[pltpu.VMEM((1, LANES), F32)],
        compiler_params=_params(("arbitrary",), 2 * tm * D * 4 + 4 * tm * D * 4),
        name="forget_cumsum",
    )(h, gain.reshape(1, D), scale, shift, w_pad, b_pad)


def _router_kernel(h_ref, gain_ref, scale_ref, shift_ref, w_ref, u_ref, route_ref, w0_ref, w1_ref,
                   count_ref, carry_ref, *, n_experts):
    @pl.when(pl.program_id(0) == 0)
    def _():
        carry_ref[...] = jnp.zeros_like(carry_ref)

    u, logits = _small_proj(h_ref, gain_ref, scale_ref, shift_ref, w_ref)
    u_ref[...] = u.reshape(u_ref.shape).astype(u_ref.dtype)
    tm = logits.shape[0]
    lane = lax.broadcasted_iota(jnp.int32, (tm, LANES), 1)
    neg = -jnp.inf
    l0 = jnp.where(lane < n_experts, logits, neg)
    m0 = jnp.max(l0, axis=-1, keepdims=True)
    i0 = jnp.min(jnp.where(l0 == m0, lane, LANES), axis=-1, keepdims=True)
    l1 = jnp.where(lane == i0, neg, l0)
    m1 = jnp.max(l1, axis=-1, keepdims=True)
    i1 = jnp.min(jnp.where(l1 == m1, lane, LANES), axis=-1, keepdims=True)
    e = jnp.exp(m1 - m0)
    wgt0 = 1.0 / (1.0 + e)
    wgt1 = e / (1.0 + e)
    onehot = jnp.where(lane == i0, 1.0, jnp.where(lane == i1, 1.0, 0.0))
    row = lax.broadcasted_iota(jnp.int32, (tm, tm), 0)
    col = lax.broadcasted_iota(jnp.int32, (tm, tm), 1)
    tri = jnp.where(col < row, 1.0, 0.0).astype(BF16)
    before = jnp.dot(tri, onehot.astype(BF16), preferred_element_type=F32) + carry_ref[...]
    rank0 = jnp.sum(jnp.where(lane == i0, before, 0.0), axis=-1, keepdims=True)
    rank1 = jnp.sum(jnp.where(lane == i1, before, 0.0), axis=-1, keepdims=True)
    total = carry_ref[...] + jnp.sum(onehot, axis=0, keepdims=True)
    carry_ref[...] = total
    count_ref[...] = jnp.broadcast_to(total, count_ref.shape)
    route = jnp.where(lane == 0, i0.astype(F32),
                      jnp.where(lane == 1, i1.astype(F32),
                                jnp.where(lane == 2, rank0, jnp.where(lane == 3, rank1, 0.0))))
    route_ref[...] = route
    w0_ref[...] = jnp.broadcast_to(wgt0, w0_ref.shape)
    w1_ref[...] = jnp.broadcast_to(wgt1, w1_ref.shape)


def _router(h, gain, scale, shift, w_router, seq, *, tm=512):
    T, D = h.shape
    E = w_router.shape[1]
    tm = _tile(seq, tm)
    rows_per_b = seq // tm
    w_pad = jnp.zeros((D, LANES), F32).at[:, :E].set(w_router)
    kern = functools.partial(_router_kernel, n_experts=E)
    row_spec = pl.BlockSpec((tm, LANES), lambda i: (i, 0))
    row_shape = jax.ShapeDtypeStruct((T, LANES), F32)
    return pl.pallas_call(
        kern,
        grid=(T // tm,),
        in_specs=[
            pl.BlockSpec((tm, D), lambda i: (i, 0)),
            pl.BlockSpec((1, D), lambda i: (0, 0)),
            pl.BlockSpec((None, 1, D), lambda i: (i // rows_per_b, 0, 0)),
            pl.BlockSpec((None, 1, D), lambda i: (i // rows_per_b, 0, 0)),
            pl.BlockSpec((D, LANES), lambda i: (0, 0)),
        ],
        out_specs=[
            pl.BlockSpec((tm,) + _row_tile(D), lambda i: (i, 0, 0)),
            row_spec, row_spec, row_spec,
            pl.BlockSpec((8, LANES), lambda i: (0, 0)),
        ],
        out_shape=[
            jax.ShapeDtypeStruct((T,) + _row_tile(D), BF16),
            row_shape, row_shape, row_shape,
            jax.ShapeDtypeStruct((8, LANES), F32),
        ],
        scratch_shapes=[pltpu.VMEM((1, LANES), F32)],
        compiler_params=_params(("arbitrary",), 2 * tm * D * 4 + 2 * tm * D * 2 + 4 * tm * D * 4),
        name="moe_router",
    )(h, gain.reshape(1, D), scale, shift, w_pad)


def _fox_kernel(ti_ref, tj_ref, q_ref, k_ref, e_ref, v_ref, o_ref, kp_ref, vp_ref, m_ref, acc_ref,
                *bufs, tq, n_off, n_diag):
    lane = lax.broadcasted_iota(jnp.int32, k_ref.shape, 1)
    kp_ref[:FOX_DH, :] = k_ref[...].T
    kp_ref[FOX_DH:, :] = e_ref[...].T
    vp_ref[:, :FOX_DH] = v_ref[...]
    vp_ref[:, FOX_DH:] = jnp.where(lane == 0, 1.0, 0.0).astype(BF16)
    m_ref[...] = jnp.full(m_ref.shape, -jnp.inf, F32)
    acc_ref[...] = jnp.zeros_like(acc_ref)

    LAG = FOX_PIPELINE_LAG
    nbuf = 2 * LAG
    lane = lax.broadcasted_iota(jnp.int32, (tq, FOX_DH), 1)
    ones = jnp.where(lane < 3, 1.0, 0.0).astype(BF16)
    s_refs, p_refs, a_refs = bufs[:nbuf], bufs[nbuf:2 * nbuf], bufs[2 * nbuf:]

    def tile_rows(idx_ref, t):
        return pl.ds(pl.multiple_of(idx_ref[t] * tq, tq), tq)

    def logits(t, par):
        qp = jnp.concatenate([q_ref[tile_rows(ti_ref, t), :], ones], axis=-1)
        kp = kp_ref[:, tile_rows(tj_ref, t)]
        s_refs[par][...] = jnp.dot(qp, kp, preferred_element_type=F32)

    def softmax(t, par, diag):
        rows = tile_rows(ti_ref, t)
        s = s_refs[par][...]
        if diag:
            row = lax.broadcasted_iota(jnp.int32, (tq, tq), 0)
            col = lax.broadcasted_iota(jnp.int32, (tq, tq), 1)
            s = jnp.where(col <= row, s, -jnp.inf)
        m = m_ref[rows, :]
        m_new = jnp.maximum(m, jnp.max(s, axis=-1, keepdims=True))
        m_ref[rows, :] = m_new
        a_refs[par][...] = jnp.exp2(m - m_new)
        p_refs[par][...] = jnp.exp2(s - m_new).astype(BF16)

    def values(t, par, diag):
        rows = tile_rows(ti_ref, t)
        vp = vp_ref[tile_rows(tj_ref, t), :]
        acc = a_refs[par][...] * acc_ref[rows, :] + jnp.dot(p_refs[par][...], vp,
                                                            preferred_element_type=F32)
        if diag:
            o_ref[rows, :] = (acc[:, :FOX_DH] / acc[:, FOX_DH:FOX_DH + 1]).astype(o_ref.dtype)
        else:
            acc_ref[rows, :] = acc

    n = n_off + n_diag
    last = n - 1 + 2 * LAG

    def plan(u):
        tv, ts = u - 2 * LAG, u - LAG
        return (0 <= tv < n, tv >= n_off), (0 <= ts < n, ts >= n_off), u < n

    def step(u, slot, stages):
        (do_v, diag_v), (do_s, diag_s), do_l = stages
        if do_v:
            values(u - 2 * LAG, slot, diag_v)
        if do_s:
            softmax(u - LAG, (slot + LAG) % nbuf, diag_s)
        if do_l:
            logits(u, slot)

    def uniform(u0):
        plans = {plan(u0 + g) for g in range(nbuf)}
        stages = next(iter(plans))
        full = stages[0][0] and stages[1][0] and stages[2]
        return stages if len(plans) == 1 and full and u0 + nbuf - 1 <= last else None

    u = 0
    while u <= last:
        stages = uniform(u) if u % nbuf == 0 else None
        if stages is None:
            step(u, u % nbuf, plan(u))
            u += 1
            continue
        count = 1
        while uniform(u + count * nbuf) == stages:
            count += 1

        def group(d, c, base=u, stages=stages):
            for g in range(nbuf):
                step(base + nbuf * d + g, g, stages)
            return c

        lax.fori_loop(0, count, group, 0)
        u += count * nbuf


def _fox_attention(q, kv, e, batch, seq, heads, *, tq=512):
    T = q.shape[0]
    tq = _tile(seq, tq)
    nq = seq // tq
    below = [(i, j) for i in range(nq) for j in range(i)]
    tiles = below + [(i, i) for i in range(nq)]
    ti = jnp.array([t[0] for t in tiles], jnp.int32)
    tj = jnp.array([t[1] for t in tiles], jnp.int32)
    kern = functools.partial(_fox_kernel, tq=tq, n_off=len(below), n_diag=nq)
    nbuf = 2 * FOX_PIPELINE_LAG
    grid_spec = pltpu.PrefetchScalarGridSpec(
        num_scalar_prefetch=2,
        grid=(batch, heads),
        in_specs=[
            pl.BlockSpec((seq, FOX_DH), lambda b, h, ti, tj: (b, h)),
            pl.BlockSpec((seq, FOX_DH), lambda b, h, ti, tj: (b, h)),
            pl.BlockSpec((seq, FOX_DH), lambda b, h, ti, tj: (b, h)),
            pl.BlockSpec((seq, FOX_DH), lambda b, h, ti, tj: (b, heads + h)),
        ],
        out_specs=pl.BlockSpec((seq, FOX_DH), lambda b, h, ti, tj: (b, h)),
        scratch_shapes=([pltpu.VMEM((2 * FOX_DH, seq), BF16), pltpu.VMEM((seq, 2 * FOX_DH), BF16),
                         pltpu.VMEM((seq, 1), F32), pltpu.VMEM((seq, 2 * FOX_DH), F32)]
                        + [pltpu.VMEM((tq, tq), F32)] * nbuf + [pltpu.VMEM((tq, tq), BF16)] * nbuf
                        + [pltpu.VMEM((tq, 1), F32)] * nbuf),
    )
    vmem = 14 * seq * FOX_DH * 2 + seq * (LANES + 2 * FOX_DH) * 4 + nbuf * tq * (6 * tq + 4 * LANES)
    return pl.pallas_call(
        kern,
        grid_spec=grid_spec,
        out_shape=jax.ShapeDtypeStruct((T, heads * FOX_DH), BF16),
        compiler_params=_params(("arbitrary", "arbitrary"), vmem),
        name="fox_attention",
    )(ti, tj, q, kv, e, kv)


def _swiglu_step(u, w1_ref, w3_ref, w2_ref, acc_ref):
    a = jnp.dot(u, w1_ref[...].astype(BF16), preferred_element_type=F32)
    g = jnp.dot(u, w3_ref[...].astype(BF16), preferred_element_type=F32)
    mid = (a * jax.nn.sigmoid(a) * g).astype(BF16)
    acc_ref[...] += jnp.dot(mid, w2_ref[...].astype(BF16), preferred_element_type=F32)


def _dense_ffn_kernel(h_ref, gain_ref, scale_ref, shift_ref, gate_ref, w1_ref, w3_ref, w2_ref,
                      o_ref, u_ref, acc_ref):
    f = pl.program_id(1)

    @pl.when(f == 0)
    def _():
        u_ref[...] = _norm_mod(h_ref[...], gain_ref[...], scale_ref[...], shift_ref[...]).astype(BF16)
        acc_ref[...] = jnp.zeros_like(acc_ref)

    _swiglu_step(u_ref[...], w1_ref, w3_ref, w2_ref, acc_ref)

    @pl.when(f == pl.num_programs(1) - 1)
    def _():
        o_ref[...] = h_ref[...] + gate_ref[...] * acc_ref[...]


def _dense_ffn(h, gain, scale, shift, gate, w13, w2, seq, *, tm=512, tf=512):
    T, D = h.shape
    F = w2.shape[0]
    tm, tf = _tile(seq, tm), _tile(F, tf)
    nf = F // tf
    rows_per_b = seq // tm
    vec = pl.BlockSpec((None, 1, D), lambda i, f: (i // rows_per_b, 0, 0))
    vmem = 4 * tm * D * 4 + tm * D * 2 + tm * D * 4 + 2 * 3 * D * tf * 2
    return pl.pallas_call(
        _dense_ffn_kernel,
        grid=(T // tm, nf),
        in_specs=[
            pl.BlockSpec((tm, D), lambda i, f: (i, 0)),
            pl.BlockSpec((1, D), lambda i, f: (0, 0)),
            vec, vec, vec,
            pl.BlockSpec((D, tf), lambda i, f: (0, f)),
            pl.BlockSpec((D, tf), lambda i, f: (0, nf + f)),
            pl.BlockSpec((tf, D), lambda i, f: (f, 0)),
        ],
        out_specs=pl.BlockSpec((tm, D), lambda i, f: (i, 0)),
        out_shape=jax.ShapeDtypeStruct((T, D), F32),
        scratch_shapes=[pltpu.VMEM((tm, D), BF16), pltpu.VMEM((tm, D), F32)],
        compiler_params=_params(("arbitrary", "arbitrary"), vmem),
        name="dense_swiglu",
    )(h, gain.reshape(1, D), scale, shift, gate, w13, w13, w2)


def _expert_ffn_kernel(te_ref, tr_ref, x_ref, w1_ref, w3_ref, w2_ref, o_ref, x2_ref, acc_ref):
    i, f = pl.program_id(0), pl.program_id(1)
    rows = tr_ref[i]
    quarter = x2_ref.shape[0] // 4

    @pl.when(f == 0)
    def _():
        x2_ref[...] = x_ref[...].reshape(x2_ref.shape)
        acc_ref[...] = jnp.zeros_like(acc_ref)

    for q in range(1, 5):
        @pl.when(jnp.logical_and(rows > (q - 1) * quarter, rows <= q * quarter))
        def _(n=q * quarter):
            _swiglu_step(x2_ref[:n, :], w1_ref, w3_ref, w2_ref, acc_ref.at[pl.ds(0, n), :])

    @pl.when(f == pl.num_programs(1) - 1)
    def _():
        o_ref[...] = acc_ref[...].astype(o_ref.dtype).reshape(o_ref.shape)


def _expert_ffn(xs, tile_expert, tile_rows, w13, w2, layer, *, tm, tf=256):
    P = xs.shape[0]
    row_tile = xs.shape[1:]
    _, E, F, D = w2.shape
    tf = _tile(F, tf)
    nf = F // tf

    def fidx(i, f, nu):
        return jnp.where(nu[i] > 0, f, nf - 1)

    grid_spec = pltpu.PrefetchScalarGridSpec(
        num_scalar_prefetch=2,
        grid=(P // tm, nf),
        in_specs=[
            pl.BlockSpec((tm,) + row_tile, lambda i, f, te, nu: (i, 0, 0)),
            pl.BlockSpec((None, None, D, tf), lambda i, f, te, nu: (layer, te[i], 0, fidx(i, f, nu))),
            pl.BlockSpec((None, None, D, tf), lambda i, f, te, nu: (layer, te[i], 0, nf + fidx(i, f, nu))),
            pl.BlockSpec((None, None, tf, D), lambda i, f, te, nu: (layer, te[i], fidx(i, f, nu), 0)),
        ],
        out_specs=pl.BlockSpec((tm,) + row_tile, lambda i, f, te, nu: (i, 0, 0)),
        scratch_shapes=[pltpu.VMEM((tm, D), BF16), pltpu.VMEM((tm, D), F32)],
    )
    w_bytes = jnp.dtype(w2.dtype).itemsize
    vmem = 5 * tm * D * 2 + tm * D * 4 + 3 * D * tf * (2 * w_bytes + 2) + 4 * tm * tf * 4
    return pl.pallas_call(
        _expert_ffn_kernel,
        grid_spec=grid_spec,
        out_shape=jax.ShapeDtypeStruct((P,) + row_tile, BF16),
        compiler_params=_params(("arbitrary", "arbitrary"), vmem),
        name="expert_swiglu",
    )(tile_expert, tile_rows, xs, w13, w13, w2)


def _dispatch_kernel(pos_ref, u_ref, xs_in_ref, xs_ref, sem, *, chunk):
    del xs_in_ref

    def copy(t, k):
        return pltpu.make_async_copy(u_ref.at[t], xs_ref.at[pos_ref[TOP_K * t + k]], sem)

    def start(t, c):
        for k in range(TOP_K):
            copy(t, k).start(priority=k % 2)
        return c

    def wait(t, c):
        for k in range(TOP_K):
            copy(t, k).wait()
        return c

    lax.fori_loop(0, chunk, start, 0, unroll=4)
    lax.fori_loop(0, chunk, wait, 0, unroll=4)


def _dispatch(u3, pos_flat, n_slots, *, chunk=512):
    T = u3.shape[0]
    chunk = _tile(T, chunk)
    xs0 = jnp.zeros((n_slots,) + u3.shape[1:], u3.dtype)
    kern = functools.partial(_dispatch_kernel, chunk=chunk)
    return pl.pallas_call(
        kern,
        grid=(T // chunk,),
        in_specs=[
            pl.BlockSpec((TOP_K * chunk,), lambda i: (i,), memory_space=pltpu.SMEM),
            pl.BlockSpec((chunk,) + u3.shape[1:], lambda i: (i, 0, 0)),
            pl.BlockSpec(memory_space=pl.ANY),
        ],
        out_specs=pl.BlockSpec(memory_space=pl.ANY),
        out_shape=jax.ShapeDtypeStruct(xs0.shape, xs0.dtype),
        scratch_shapes=[pltpu.SemaphoreType.DMA(())],
        input_output_aliases={2: 0},
        compiler_params=_params(("arbitrary",), 4 * chunk * u3.shape[1] * u3.shape[2] * 2),
        name="moe_dispatch",
    )(pos_flat, u3, xs0)


def _combine_kernel(pos_ref, ys_ref, w0_ref, w1_ref, h_ref, gate_ref, o_ref, buf_ref, sem, *, chunk):
    def copy(t, k):
        return pltpu.make_async_copy(ys_ref.at[pos_ref[TOP_K * t + k]], buf_ref.at[k, t], sem)

    def start(t, c):
        for k in range(TOP_K):
            copy(t, k).start(priority=k % 2)
        return c

    def wait(t, c):
        for k in range(TOP_K):
            copy(t, k).wait()
        return c

    lax.fori_loop(0, chunk, start, 0, unroll=4)
    lax.fori_loop(0, chunk, wait, 0, unroll=4)
    y = w0_ref[...] * buf_ref[0].astype(F32) + w1_ref[...] * buf_ref[1].astype(F32)
    o_ref[...] = h_ref[...] + gate_ref[...] * y.reshape(o_ref.shape)


def _combine(ys3, pos_flat, w0, w1, h, gate, seq, *, chunk=512):
    T, D = h.shape
    tile = ys3.shape[1:]
    chunk = _tile(seq, chunk)
    rows_per_b = seq // chunk
    kern = functools.partial(_combine_kernel, chunk=chunk)
    wspec = pl.BlockSpec((chunk, 1, LANES), lambda i: (i, 0, 0))
    return pl.pallas_call(
        kern,
        grid=(T // chunk,),
        in_specs=[
            pl.BlockSpec((TOP_K * chunk,), lambda i: (i,), memory_space=pltpu.SMEM),
            pl.BlockSpec(memory_space=pl.ANY),
            wspec, wspec,
            pl.BlockSpec((chunk, D), lambda i: (i, 0)),
            pl.BlockSpec((None, 1, D), lambda i: (i // rows_per_b, 0, 0)),
        ],
        out_specs=pl.BlockSpec((chunk, D), lambda i: (i, 0)),
        out_shape=jax.ShapeDtypeStruct((T, D), F32),
        scratch_shapes=[pltpu.VMEM((TOP_K, chunk) + tile, BF16), pltpu.SemaphoreType.DMA(())],
        compiler_params=_params(("arbitrary",), 10 * chunk * D * 4),
        name="moe_combine",
    )(pos_flat, ys3, w0.reshape(T, 1, LANES), w1.reshape(T, 1, LANES), h, gate)


def _moe_layer(h, gain, scale, shift, gate, w_router, w13, w2, layer, seq, *, tm=1024):
    T, D = h.shape
    E = w_router.shape[1]
    u3, route, w0, w1, counts = _router(h, gain, scale, shift, w_router, seq)

    expert = route[:, :TOP_K].astype(jnp.int32)
    rank = route[:, TOP_K:2 * TOP_K].astype(jnp.int32)
    count = counts[0, :E].astype(jnp.int32)
    padded = ((count + tm - 1) // tm) * tm
    ends = jnp.cumsum(padded)
    starts = ends - padded
    pos_flat = (starts[expert] + rank).reshape(-1)
    n_tiles = (TOP_K * T) // tm + E
    tile_start = jnp.arange(n_tiles, dtype=jnp.int32) * tm
    last_used = jnp.maximum(ends[-1] // tm - 1, 0)
    tile_expert = jnp.minimum(jnp.searchsorted(ends, tile_start, side="right"), E - 1).astype(jnp.int32)
    tile_rows = jnp.clip(starts[tile_expert] + count[tile_expert] - tile_start, 0, tm).astype(jnp.int32)
    tile_rows = jnp.where(tile_start < ends[-1], tile_rows, 0)
    tile_expert = jnp.where(tile_start < ends[-1], tile_expert, tile_expert[last_used])

    xs3 = _dispatch(u3, pos_flat, n_tiles * tm)
    ys3 = _expert_ffn(xs3, tile_expert, tile_rows, w13, w2, layer, tm=tm)
    return _combine(ys3, pos_flat, w0, w1, h, gate, seq)


def kernel(x, c, norm_gain, mod_w, mod_b, ret_w_in, ret_w_out, fox_w_q, fox_q_gain, fox_w_out,
           kv_norm_gain, kv_mod_w, kv_mod_b, kv_w, kv_forget_bias, kv_k_gain,
           ffn_w13, ffn_w2, moe_router, moe_w13, moe_w2):
    B, S, D = x.shape
    T = B * S
    depth = mod_w.shape[0]
    n_ret = ret_w_in.shape[0]
    ret_heads = ret_w_out.shape[1] // RET_DV
    fox_heads = fox_w_q.shape[2] // FOX_DH
    fox_width = fox_heads * FOX_DH

    c8 = jnp.zeros((8, D), F32).at[:B].set(c)
    mod = _mod_call(c8, mod_w, mod_b)[:, :B]
    kv_mod = _mod_call(c8, kv_mod_w[None], kv_mod_b[None])[0, :B]

    def vecs(m, n):
        return [v.reshape(B, 1, D) for v in jnp.split(m, n, axis=-1)]

    half = RET_DK // 2
    inv = 1.0 / (ROPE_BASE ** (jnp.arange(half, dtype=F32) / half))
    ang = jnp.arange(S, dtype=F32)[:, None] * inv[None, :]
    rope = (jnp.cos(ang), jnp.sin(ang))

    h = x.reshape(T, D)
    kv_sh = decay_sh = None
    for i in range(depth):
        sh1, sc1, g1, sh2, sc2, g2 = vecs(mod[i], 6)
        if i < n_ret:
            proj = _nm_matmul(h, norm_gain[i, 0], sc1, sh1, ret_w_in[i].astype(BF16), S)
            y = _retention(proj, rope, B, S, ret_heads)
            h = _proj_residual(y, ret_w_out[i].astype(BF16), h, g1, S)
        else:
            j = i - n_ret
            q = _nm_matmul(h, norm_gain[i, 0], sc1, sh1, fox_w_q[j].astype(BF16), S,
                           head_gain=fox_q_gain[j], special_cols=fox_width,
                           head_scale=LOG2E * FOX_DH ** -0.5)
            y = _fox_attention(q, kv_sh, decay_sh, B, S, fox_heads)
            h = _proj_residual(y, fox_w_out[j].astype(BF16), h, g1, S, tn=1024)
        if i % 2 == 0:
            h = _dense_ffn(h, norm_gain[i, 1], sc2, sh2, g2,
                           ffn_w13[i // 2].astype(BF16), ffn_w2[i // 2].astype(BF16), S)
        else:
            h = _moe_layer(h, norm_gain[i, 1], sc2, sh2, g2, moe_router[i // 2], moe_w13, moe_w2, i // 2, S)
        if i == n_ret - 1:
            ksh, ksc = vecs(kv_mod, 2)
            kv_sh = _nm_matmul(h, kv_norm_gain, ksc, ksh, kv_w[:, :2 * fox_width].astype(BF16), S,
                               head_gain=kv_k_gain, special_cols=fox_width)
            decay_sh = _forget_cumsum(h, kv_norm_gain, ksc, ksh, kv_w[:, 2 * fox_width:], kv_forget_bias, S)
    return h.reshape(B, S, D)
```
